```python
import jax, jax.numpy as jnp
from jax import lax
import numpy as np

D_MODEL = 1024
BATCH = 8
SEQ = 4096
DEPTH = 1

CHUNK = 64
Q_BLOCK = 128
EPS = 1e-6
NEG_INF = -1e30

MLA_HEADS = 8
MLA_NOPE_DIM = 64
MLA_ROPE_DIM = 32
MLA_V_DIM = 64
MLA_Q_RANK = 256
MLA_KV_RANK = 128
ROPE_THETA = 10000.0
MLA_WIDTH = MLA_HEADS * MLA_V_DIM

FOX_HEADS = 8
FOX_HEAD_DIM = 64
FOX_WIDTH = FOX_HEADS * FOX_HEAD_DIM
FORGET_BIAS_MEAN = 3.0

N_GROUPS = 4
EXPERTS_PER_GROUP = 8
N_EXPERTS = N_GROUPS * EXPERTS_PER_GROUP
TOP_K = 2
EXPERT_FF = 256

IN_SPLITS = [MLA_Q_RANK, MLA_KV_RANK, MLA_ROPE_DIM,
             FOX_WIDTH, FOX_WIDTH, FOX_WIDTH, FOX_HEADS,
             D_MODEL, D_MODEL]
D_IN = int(sum(IN_SPLITS))
SPLIT_IDX = [int(v) for v in np.cumsum(IN_SPLITS)[:-1]]

kernel_name = "hybrid_mla_fox_hmoe_block"


def rms_norm(x, g):
    xf = x.astype(jnp.float32)
    y = xf * lax.rsqrt(jnp.mean(xf * xf, axis=-1, keepdims=True) + EPS)
    return (y * g.astype(jnp.float32)).astype(x.dtype)


def rope(x, positions):
    d = x.shape[-1]
    half = d // 2
    inv_freq = ROPE_THETA ** (-jnp.arange(half, dtype=jnp.float32) / half)
    ang = positions.astype(jnp.float32)[:, :, None, None] * inv_freq
    cos, sin = jnp.cos(ang), jnp.sin(ang)
    xf = x.astype(jnp.float32)
    x1, x2 = xf[..., :half], xf[..., half:]
    out = jnp.concatenate([x1 * cos - x2 * sin, x2 * cos + x1 * sin], axis=-1)
    return out.astype(x.dtype)


def blocked_attention(q, k, v, scale, decay=None):
    S = q.shape[1]
    outs = []
    for i in range(S // Q_BLOCK):
        q0, q1 = i * Q_BLOCK, (i + 1) * Q_BLOCK
        logits = jnp.einsum('bqhd,bkhd->bhqk', q[:, q0:q1], k[:, :q1]).astype(jnp.float32) * scale
        q_pos = jnp.arange(q0, q1)[:, None]
        k_pos = jnp.arange(q1)[None, :]
        if decay is None:
            allowed = (k_pos // CHUNK) <= (q_pos // CHUNK)
        else:
            logits = logits + decay[:, :, q0:q1, None] - decay[:, :, None, :q1]
            allowed = k_pos <= q_pos
        logits = jnp.where(allowed, logits, NEG_INF)
        probs = jax.nn.softmax(logits, axis=-1).astype(v.dtype)
        outs.append(jnp.einsum('bhqk,bkhd->bqhd', probs, v[:, :q1]))
    return jnp.concatenate(outs, axis=1)


def hybrid_mixer(xn, positions, w_in, b_forget, g_cq, g_ckv, w_uq, w_uk, w_uv,
                 w_o_mla, w_o_fox, w_out):
    B, S, _ = xn.shape
    z = xn @ w_in
    c_q, c_kv, k_r, fq, fk, fv, f_logit, gate_a, gate_b = jnp.split(z, SPLIT_IDX, axis=-1)

    q = (rms_norm(c_q, g_cq) @ w_uq).reshape(B, S, MLA_HEADS, MLA_NOPE_DIM + MLA_ROPE_DIM)
    q = jnp.concatenate([q[..., :MLA_NOPE_DIM], rope(q[..., MLA_NOPE_DIM:], positions)], axis=-1)
    ckv = rms_norm(c_kv, g_ckv)
    k_nope = (ckv @ w_uk).reshape(B, S, MLA_HEADS, MLA_NOPE_DIM)
    v_a = (ckv @ w_uv).reshape(B, S, MLA_HEADS, MLA_V_DIM)
    k_rope = rope(k_r[:, :, None, :], positions)
    k_a = jnp.concatenate(
        [k_nope, jnp.broadcast_to(k_rope, (B, S, MLA_HEADS, MLA_ROPE_DIM))], axis=-1)
    o_a = blocked_attention(q, k_a, v_a, (MLA_NOPE_DIM + MLA_ROPE_DIM) ** -0.5)
    y_a = o_a.reshape(B, S, MLA_WIDTH) @ w_o_mla

    q_b = fq.reshape(B, S, FOX_HEADS, FOX_HEAD_DIM)
    k_b = fk.reshape(B, S, FOX_HEADS, FOX_HEAD_DIM)
    v_b = fv.reshape(B, S, FOX_HEADS, FOX_HEAD_DIM)
    log_f = jax.nn.log_sigmoid(f_logit.astype(jnp.float32) + b_forget.astype(jnp.float32))
    decay = jnp.transpose(jnp.cumsum(log_f, axis=1), (0, 2, 1))
    o_b = blocked_attention(q_b, k_b, v_b, FOX_HEAD_DIM ** -0.5, decay)
    y_b = o_b.reshape(B, S, FOX_WIDTH) @ w_o_fox

    mixed = jax.nn.sigmoid(gate_a) * y_a + jax.nn.sigmoid(gate_b) * y_b
    return mixed @ w_out


def hierarchical_moe(hn, w_group, b_group, w_router, b_router, w_e_gate, w_e_up, w_e_down):
    B, S, D = hn.shape
    t = hn.reshape(B * S, D)
    p_group = jax.nn.softmax((t @ w_group).astype(jnp.float32) + b_group.astype(jnp.float32), axis=-1)
    g_p, g_idx = lax.top_k(p_group, 1)
    exp_logits = ((t @ w_router).astype(jnp.float32) + b_router.astype(jnp.float32)
                  ).reshape(-1, N_GROUPS, EXPERTS_PER_GROUP)
    sel = jnp.take_along_axis(exp_logits, g_idx[:, :, None], axis=1)[:, 0]
    e_p, e_idx = lax.top_k(jax.nn.softmax(sel, axis=-1), TOP_K)
    e_p = e_p / jnp.sum(e_p, axis=-1, keepdims=True)
    weights = g_p * e_p
    expert_id = g_idx * EXPERTS_PER_GROUP + e_idx
    combine = jnp.sum(jax.nn.one_hot(expert_id, N_EXPERTS, dtype=jnp.float32)
                      * weights[..., None], axis=1)
    y = jnp.zeros((B * S, D), jnp.float32)
    for e in range(N_EXPERTS):
        a = jax.nn.silu(t @ w_e_gate[e]) * (t @ w_e_up[e])
        y = y + combine[:, e:e + 1] * (a @ w_e_down[e]).astype(jnp.float32)
    return y.reshape(B, S, D).astype(hn.dtype)


def setup_inputs(seed: int = 0) -> dict:
    key = jax.random.key(seed)
    ks = jax.random.split(key, 24)
    f32 = jnp.float32

    def w(k, shape, fan_in):
        return jax.random.normal(k, shape, f32) * (fan_in ** -0.5)

    def gain(k, n):
        return 1.0 + 0.02 * jax.random.normal(k, (n,), f32)

    x = jax.random.normal(ks[0], (BATCH, SEQ, D_MODEL), f32)
    offset = jax.random.randint(ks[1], (BATCH, 1), 0, 4096, dtype=jnp.int32)
    positions = offset + jnp.arange(SEQ, dtype=jnp.int32)[None, :]
    return {
        "x": x,
        "positions": positions,
        "g_mix": gain(ks[2], D_MODEL),
        "w_in": w(ks[3], (D_MODEL, D_IN), D_MODEL),
        "b_forget": FORGET_BIAS_MEAN + 0.5 * jax.random.normal(ks[4], (FOX_HEADS,), f32),
        "g_cq": gain(ks[5], MLA_Q_RANK),
        "g_ckv": gain(ks[6], MLA_KV_RANK),
        "w_uq": w(ks[7], (MLA_Q_RANK, MLA_HEADS * (MLA_NOPE_DIM + MLA_ROPE_DIM)), MLA_Q_RANK),
        "w_uk": w(ks[8], (MLA_KV_RANK, MLA_HEADS * MLA_NOPE_DIM), MLA_KV_RANK),
        "w_uv": w(ks[9], (MLA_KV_RANK, MLA_HEADS * MLA_V_DIM), MLA_KV_RANK),
        "w_o_mla": w(ks[10], (MLA_WIDTH, D_MODEL), MLA_WIDTH),
        "w_o_fox": w(ks[11], (FOX_WIDTH, D_MODEL), FOX_WIDTH),
        "w_out": w(ks[12], (D_MODEL, D_MODEL), D_MODEL),
        "g_ffn": gain(ks[13], D_MODEL),
        "w_group": w(ks[14], (D_MODEL, N_GROUPS), D_MODEL),
        "b_group": 0.01 * jax.random.normal(ks[15], (N_GROUPS,), f32),
        "w_router": w(ks[16], (D_MODEL, N_EXPERTS), D_MODEL),
        "b_router": 0.01 * jax.random.normal(ks[17], (N_EXPERTS,), f32),
        "w_e_gate": w(ks[18], (N_EXPERTS, D_MODEL, EXPERT_FF), D_MODEL),
        "w_e_up": w(ks[19], (N_EXPERTS, D_MODEL, EXPERT_FF), D_MODEL),
        "w_e_down": w(ks[20], (N_EXPERTS, EXPERT_FF, D_MODEL), EXPERT_FF),
        "g_final": gain(ks[21], D_MODEL),
    }


def reference(x, positions, g_mix, w_in, b_forget, g_cq, g_ckv, w_uq, w_uk, w_uv,
              w_o_mla, w_o_fox, w_out, g_ffn, w_group, b_group, w_router, b_router,
              w_e_gate, w_e_up, w_e_down, g_final):
    h = x
    for _ in range(DEPTH):
        h = h + hybrid_mixer(rms_norm(h, g_mix), positions, w_in, b_forget, g_cq, g_ckv,
                             w_uq, w_uk, w_uv, w_o_mla, w_o_fox, w_out)
        h = h + hierarchical_moe(rms_norm(h, g_ffn), w_group, b_group, w_router, b_router,
                                 w_e_gate, w_e_up, w_e_down)
    return rms_norm(h, g_final)
```

```python
import functools

import jax
import jax.numpy as jnp
from jax import lax
from jax.experimental import pallas as pl
from jax.experimental.pallas import tpu as pltpu

F32 = jnp.float32
BF16 = jnp.bfloat16

D_MODEL = 1024
CHUNK = 64
EPS = 1e-6
NEG_INF = -1e30
LOG2E = 1.4426950408889634

MLA_HEADS = 8
MLA_NOPE_DIM = 64
MLA_ROPE_DIM = 32
MLA_V_DIM = 64
MLA_Q_RANK = 256
MLA_KV_RANK = 128
ROPE_THETA = 10000.0
FOX_HEADS = 8
FOX_HEAD_DIM = 64
N_GROUPS = 4
EXPERTS_PER_GROUP = 8
N_EXPERTS = 32
EXPERT_FF = 256

LANES = 128
HEAD_SLOT = LANES
AUG_SLOT = 8
VMEM_LIMIT_BYTES = 56 * 1024 * 1024

_SEG_WIDTHS = dict(cq=MLA_Q_RANK, ckv=MLA_KV_RANK, krm=HEAD_SLOT, krs=HEAD_SLOT,
                   fq=512, fk=512, fv=512, flog=LANES, ga=D_MODEL, gb=D_MODEL)
_SEG = {}
_off = 0
for _name, _w in _SEG_WIDTHS.items():
    _SEG[_name] = (_off, _off + _w)
    _off += _w
W1_COLS = _off


def _split3(v):
    hi = v.astype(BF16).astype(F32)
    r = v - hi
    mid = r.astype(BF16).astype(F32)
    lo = (r - mid).astype(BF16).astype(F32)
    return hi, mid, lo


def _proj_kernel(x_ref, pos_ref, gmix_ref, w1_ref, gcq_ref, gckv_ref, wqm_ref, wqs_ref, wuk_ref,
                 wuv_ref, bf_ref, freq_ref, tri_ref,
                 q_ref, k_ref, vt_ref, fq_ref, fk_ref, qaug_ref, kaug_ref, gates_ref,
                 carry_ref, *, tiles_per_seq, tk, scale_a, scale_b):
    i = pl.program_id(0)
    tm = x_ref.shape[0]

    x = x_ref[...]
    ms = jnp.mean(x * x, axis=-1, keepdims=True)
    y = ((x * lax.rsqrt(ms + EPS)) * gmix_ref[...]).astype(BF16)

    def seg(name):
        a, b = _SEG[name]
        return jnp.dot(y, w1_ref[:, a:b], preferred_element_type=F32)

    ang = pos_ref[...].astype(F32) * freq_ref[...]
    cos = jnp.cos(ang)
    sin = jnp.sin(ang)

    cq = seg("cq")
    cqn = ((cq * lax.rsqrt(jnp.mean(cq * cq, axis=-1, keepdims=True) + EPS)) * gcq_ref[...]).astype(BF16)
    qm = jnp.dot(cqn, wqm_ref[...], preferred_element_type=F32)
    qs = jnp.dot(cqn, wqs_ref[...], preferred_element_type=F32)
    cos_q = cos * (scale_a * LOG2E)
    sin_q = sin * (scale_a * LOG2E)
    ckv = seg("ckv")
    ckvn = ((ckv * lax.rsqrt(jnp.mean(ckv * ckv, axis=-1, keepdims=True) + EPS)) * gckv_ref[...]).astype(BF16)
    knope = jnp.dot(ckvn, wuk_ref[...], preferred_element_type=F32)
    va = jnp.dot(ckvn, wuv_ref[...], preferred_element_type=F32)
    krope = seg("krm") * cos + seg("krs") * sin
    for h in range(MLA_HEADS):
        sl = slice(h * HEAD_SLOT, (h + 1) * HEAD_SLOT)
        q_ref[:, sl] = (qm[:, sl] * cos_q + qs[:, sl] * sin_q).astype(BF16)
        k_ref[:, sl] = (knope[:, sl] + krope).astype(BF16)

    fq_ref[...] = (seg("fq") * (scale_b * LOG2E)).astype(BF16)
    fk_ref[...] = seg("fk").astype(BF16)
    fv = seg("fv")
    va_t = va.T
    fv_t = fv.T
    for c in range(tm // tk):
        vt_ref[c, 0:512, :] = va_t[:, c * tk:(c + 1) * tk].astype(BF16)
        vt_ref[c, 512:1024, :] = fv_t[:, c * tk:(c + 1) * tk].astype(BF16)

    logf = jax.nn.log_sigmoid(seg("flog") + bf_ref[...])
    hi, mid, lo = _split3(logf)
    tri = tri_ref[...]
    cum = (jnp.dot(tri, hi.astype(BF16), preferred_element_type=F32)
           + jnp.dot(tri, mid.astype(BF16), preferred_element_type=F32)
           + jnp.dot(tri, lo.astype(BF16), preferred_element_type=F32))

    @pl.when(i % tiles_per_seq == 0)
    def _():
        carry_ref[...] = jnp.zeros_like(carry_ref)

    decay = cum + carry_ref[0:1, :]
    carry_ref[0:1, :] = decay[tm - 1:tm, :]
    d_hi, d_mid, d_lo = _split3(decay * LOG2E)
    lane = lax.broadcasted_iota(jnp.int32, (1, LANES), 1)
    l8 = lane % AUG_SLOT
    valid = lane < FOX_HEADS * AUG_SLOT
    one = jnp.ones_like(d_hi)
    zero = jnp.zeros_like(d_hi)
    qaug = jnp.where(l8 == 0, d_hi, jnp.where(l8 == 1, d_mid, jnp.where(l8 == 2, d_lo,
                     jnp.where(l8 < 6, one, zero))))
    kaug = jnp.where(l8 < 3, one, jnp.where(l8 == 3, -d_hi, jnp.where(l8 == 4, -d_mid,
                     jnp.where(l8 == 5, -d_lo, zero))))
    qaug_ref[...] = jnp.where(valid, qaug, zero).astype(BF16)
    kaug_ref[...] = jnp.where(valid, kaug, zero).astype(BF16)

    gates_ref[:, 0:D_MODEL] = jax.nn.sigmoid(seg("ga")).astype(BF16)
    gates_ref[:, D_MODEL:2 * D_MODEL] = jax.nn.sigmoid(seg("gb")).astype(BF16)


def _attn_kernel(*refs, fox, tq, tk, seq):
    if fox:
        q_ref, k_ref, qaug_ref, kaug_ref, vt_ref, o_ref = refs
    else:
        q_ref, k_ref, vt_ref, o_ref = refs
    pair = pl.program_id(1)
    nq = seq // tq
    per = tq // tk
    lane = lax.broadcasted_iota(jnp.int32, (1, LANES), 1)
    nt = (((1,), (1,)), ((), ()))

    def q_block(i, _):
        q0 = pl.multiple_of(i * tq, tq)
        if fox:
            qp = q_ref[pl.ds(q0, tq), :]
            qa = qaug_ref[pl.ds(q0, tq), :]
            zq = jnp.zeros_like(qp)
            qcats = []
            for h in range(2):
                qh = jnp.where(lane // FOX_HEAD_DIM == h, qp, zq)
                qah = jnp.where(lane // AUG_SLOT == 2 * pair + h, qa, zq)
                qcats.append(jnp.concatenate([qh, qah], axis=1))
        else:
            qcats = [q_ref[pl.ds(q0, tq), h * HEAD_SLOT:(h + 1) * HEAD_SLOT] for h in range(2)]

        def kv_step(j, carry, masked):
            k0 = pl.multiple_of(j * tk, tk)
            if fox:
                kcat = jnp.concatenate([k_ref[pl.ds(k0, tk), :], kaug_ref[pl.ds(k0, tk), :]], axis=1)
                kcats = [kcat, kcat]
            else:
                kcats = [k_ref[pl.ds(k0, tk), h * HEAD_SLOT:(h + 1) * HEAD_SLOT] for h in range(2)]
            if masked:
                kpos = k0 + lax.broadcasted_iota(jnp.int32, (tk, 1), 0)
                qpos = q0 + lax.broadcasted_iota(jnp.int32, (1, tq), 1)
                if fox:
                    allowed = kpos <= qpos
                else:
                    allowed = (kpos // CHUNK) <= (qpos // CHUNK)
            out = []
            for h in range(2):
                m, l, acc = carry[h]
                s = lax.dot_general(kcats[h], qcats[h], nt, preferred_element_type=F32)
                if masked:
                    s = jnp.where(allowed, s, NEG_INF)
                m_new = jnp.maximum(m, jnp.max(s, axis=0, keepdims=True))
                alpha = jnp.exp2(m - m_new)
                p = jnp.exp2(s - m_new)
                l_new = alpha * l + jnp.sum(p, axis=0, keepdims=True)
                vt = vt_ref[j, h * 64:(h + 1) * 64, :]
                pv = jnp.dot(vt, p.astype(BF16), preferred_element_type=F32)
                out.append((m_new, l_new, alpha * acc + pv))
            return tuple(out)

        init = tuple((jnp.full((1, tq), NEG_INF, F32), jnp.zeros((1, tq), F32),
                      jnp.zeros((64, tq), F32)) for _ in range(2))
        carry = lax.fori_loop(0, i * per, functools.partial(kv_step, masked=False), init)
        for d in range(per):
            carry = kv_step(i * per + d, carry, masked=True)
        o_t = jnp.concatenate([carry[h][2] / carry[h][1] for h in range(2)], axis=0)
        o_ref[pl.ds(q0, tq), :] = o_t.T.astype(BF16)
        return 0

    lax.fori_loop(0, nq, q_block, 0)


def _post_kernel(x_ref, oa_ref, ob_ref, gates_ref, woa_ref, wob_ref, wout_ref, gffn_ref,
                 wrh_ref, wrl_ref, br_ref, h_ref, hn_ref, comb_ref):
    tm = x_ref.shape[0]
    ya = jnp.dot(oa_ref[...], woa_ref[...], preferred_element_type=F32)
    yb = jnp.dot(ob_ref[...], wob_ref[...], preferred_element_type=F32)
    sa = gates_ref[:, 0:D_MODEL].astype(F32)
    sb = gates_ref[:, D_MODEL:2 * D_MODEL].astype(F32)
    mixed = (sa * ya + sb * yb).astype(BF16)
    h = x_ref[...] + jnp.dot(mixed, wout_ref[...], preferred_element_type=F32)
    h_ref[...] = h
    hn = (h * lax.rsqrt(jnp.mean(h * h, axis=-1, keepdims=True) + EPS)) * gffn_ref[...]
    hn_hi = hn.astype(BF16)
    hn_ref[...] = hn_hi
    hn_lo = (hn - hn_hi.astype(F32)).astype(BF16)

    nt = (((1,), (1,)), ((), ()))
    wrh = wrh_ref[...]
    lt = (lax.dot_general(wrh, hn_hi, nt, preferred_element_type=F32)
          + lax.dot_general(wrh, hn_lo, nt, preferred_element_type=F32)
          + lax.dot_general(wrl_ref[...], hn_hi, nt, preferred_element_type=F32)) + br_ref[...]

    row8 = lax.broadcasted_iota(jnp.int32, (EXPERTS_PER_GROUP, tm), 0).astype(F32)
    big = jnp.float32(EXPERTS_PER_GROUP)
    lg = lt[N_EXPERTS:N_EXPERTS + 8, :]
    gmax = jnp.max(lg, axis=0, keepdims=True)
    g_p = 1.0 / jnp.sum(jnp.exp(lg - gmax), axis=0, keepdims=True)
    g_idx = jnp.min(jnp.where(lg == gmax, row8, big), axis=0, keepdims=True)
    sel = jnp.zeros((EXPERTS_PER_GROUP, tm), F32)
    for g in range(N_GROUPS):
        sel = jnp.where(g_idx == g, lt[g * EXPERTS_PER_GROUP:(g + 1) * EXPERTS_PER_GROUP, :], sel)
    m1 = jnp.max(sel, axis=0, keepdims=True)
    i1 = jnp.min(jnp.where(sel == m1, row8, big), axis=0, keepdims=True)
    sel2 = jnp.where(row8 == i1, -jnp.inf, sel)
    m2 = jnp.max(sel2, axis=0, keepdims=True)
    i2 = jnp.min(jnp.where(sel2 == m2, row8, big), axis=0, keepdims=True)
    r = jnp.exp(m2 - m1)
    w1 = g_p / (1.0 + r)
    w2 = g_p * r / (1.0 + r)
    e1 = g_idx * EXPERTS_PER_GROUP + i1
    e2 = g_idx * EXPERTS_PER_GROUP + i2
    row = lax.broadcasted_iota(jnp.int32, (LANES, tm), 0).astype(F32)
    comb_t = jnp.where(row == e1, w1, 0.0) + jnp.where(row == e2, w2, 0.0)
    comb_ref[...] = comb_t.T


def _moe_kernel(hn_ref, comb_ref, h_ref, wg_ref, wu_ref, wd_ref, gfin_ref, out_ref, acc_ref):
    e = pl.program_id(1)

    @pl.when(e == 0)
    def _():
        acc_ref[...] = jnp.zeros_like(acc_ref)

    hn = hn_ref[...]
    g = jnp.dot(hn, wg_ref[0], preferred_element_type=F32)
    u = jnp.dot(hn, wu_ref[0], preferred_element_type=F32)
    a = (jax.nn.silu(g) * u).astype(BF16)
    lane = lax.broadcasted_iota(jnp.int32, (1, LANES), 1)
    c = jnp.sum(jnp.where(lane == e, comb_ref[...], 0.0), axis=-1, keepdims=True)
    acc_ref[...] += c * jnp.dot(a, wd_ref[0], preferred_element_type=F32)

    @pl.when(e == N_EXPERTS - 1)
    def _():
        hh = h_ref[...] + acc_ref[...]
        out_ref[...] = (hh * lax.rsqrt(jnp.mean(hh * hh, axis=-1, keepdims=True) + EPS)) * gfin_ref[...]


def _pad_heads(w, heads, dim):
    k = w.shape[0]
    w = w.reshape(k, heads, dim)
    return jnp.pad(w, ((0, 0), (0, 0), (0, HEAD_SLOT - dim))).reshape(k, heads * HEAD_SLOT)


def _swap_rope(w_rope):
    half = MLA_ROPE_DIM // 2
    return jnp.concatenate([-w_rope[..., half:], w_rope[..., :half]], axis=-1)


def _params(sem):
    return pltpu.CompilerParams(dimension_semantics=sem, vmem_limit_bytes=VMEM_LIMIT_BYTES)


def kernel(x, positions, g_mix, w_in, b_forget, g_cq, g_ckv, w_uq, w_uk, w_uv, w_o_mla, w_o_fox,
           w_out, g_ffn, w_group, b_group, w_router, b_router, w_e_gate, w_e_up, w_e_down, g_final):
    B, S, D = x.shape
    T = B * S
    assert D == D_MODEL
    tm = min(512, S)
    tq = min(256, S)
    tk = min(256, S)
    tm_moe = min(1024, S)
    assert S % tm == 0 and S % tq == 0 and tq % tk == 0 and tm % tk == 0 and tk % CHUNK == 0

    cq_w, ckv_w, kr_w, fq_w, fk_w, fv_w, fl_w, ga_w, gb_w = jnp.split(
        w_in, [256, 384, 416, 928, 1440, 1952, 1960, 2984], axis=1)
    z64 = jnp.zeros((D, MLA_NOPE_DIM), F32)
    z32 = jnp.zeros((D, HEAD_SLOT - MLA_NOPE_DIM - MLA_ROPE_DIM), F32)
    fl_rep = jnp.pad(jnp.repeat(fl_w, AUG_SLOT, axis=1), ((0, 0), (0, LANES - FOX_HEADS * AUG_SLOT)))
    w1 = jnp.concatenate([cq_w, ckv_w,
                          jnp.concatenate([z64, kr_w, z32], axis=1),
                          jnp.concatenate([z64, _swap_rope(kr_w), z32], axis=1),
                          fq_w, fk_w, fv_w, fl_rep, ga_w, gb_w], axis=1).astype(BF16)
    assert w1.shape[1] == W1_COLS
    bf128 = jnp.pad(jnp.repeat(b_forget.astype(F32), AUG_SLOT), (0, LANES - FOX_HEADS * AUG_SLOT))[None, :]
    dq = MLA_NOPE_DIM + MLA_ROPE_DIM
    wq3 = w_uq.reshape(MLA_Q_RANK, MLA_HEADS, dq)
    wq_main = _pad_heads(w_uq, MLA_HEADS, dq).astype(BF16)
    wq_swap3 = jnp.concatenate([jnp.zeros((MLA_Q_RANK, MLA_HEADS, MLA_NOPE_DIM), F32),
                                _swap_rope(wq3[..., MLA_NOPE_DIM:])], axis=-1)
    wq_swap = _pad_heads(wq_swap3.reshape(MLA_Q_RANK, MLA_HEADS * dq), MLA_HEADS, dq).astype(BF16)
    wuk_pad = _pad_heads(w_uk, MLA_HEADS, MLA_NOPE_DIM).astype(BF16)
    half = MLA_ROPE_DIM // 2
    inv_freq = ROPE_THETA ** (-jnp.arange(half, dtype=F32) / half)
    freq128 = jnp.concatenate([jnp.zeros((MLA_NOPE_DIM,), F32), inv_freq, inv_freq,
                               jnp.zeros((HEAD_SLOT - dq,), F32)])[None, :]
    tri = (jnp.arange(tm)[:, None] >= jnp.arange(tm)[None, :]).astype(BF16)

    x2 = x.reshape(T, D)
    pos2 = positions.reshape(T, 1).astype(jnp.int32)
    row = lambda v: v.astype(F32)[None, :]

    full = lambda shape: pl.BlockSpec(shape, lambda i: (0,) * len(shape))
    tok = lambda w: pl.BlockSpec((tm, w), lambda i: (i, 0))
    n_tiles = T // tm
    nkv = S // tk

    q, k, vt, fq, fk, qaug, kaug, gates = pl.pallas_call(
        functools.partial(_proj_kernel, tiles_per_seq=S // tm, tk=tk,
                          scale_a=float(dq) ** -0.5, scale_b=float(FOX_HEAD_DIM) ** -0.5),
        grid=(n_tiles,),
        in_specs=[tok(D), tok(1), full((1, D)), full((D, W1_COLS)), full((1, MLA_Q_RANK)),
                  full((1, MLA_KV_RANK)), full((MLA_Q_RANK, MLA_HEADS * HEAD_SLOT)),
                  full((MLA_Q_RANK, MLA_HEADS * HEAD_SLOT)), full((MLA_KV_RANK, MLA_HEADS * HEAD_SLOT)),
                  full((MLA_KV_RANK, 512)), full((1, LANES)), full((1, LANES)), full((tm, tm))],
        out_specs=[tok(1024), tok(1024),
                   pl.BlockSpec((tm // tk, 1024, tk), lambda i: (i, 0, 0)),
                   tok(512), tok(512), tok(LANES), tok(LANES), tok(2 * D)],
        out_shape=[jax.ShapeDtypeStruct((T, 1024), BF16), jax.ShapeDtypeStruct((T, 1024), BF16),
                   jax.ShapeDtypeStruct((T // tk, 1024, tk), BF16),
                   jax.ShapeDtypeStruct((T, 512), BF16), jax.ShapeDtypeStruct((T, 512), BF16),
                   jax.ShapeDtypeStruct((T, LANES), BF16), jax.ShapeDtypeStruct((T, LANES), BF16),
                   jax.ShapeDtypeStruct((T, 2 * D), BF16)],
        scratch_shapes=[pltpu.VMEM((8, LANES), F32)],
        compiler_params=_params(("arbitrary",)),
        name="proj",
    )(x2, pos2, row(g_mix), w1, row(g_cq), row(g_ckv), wq_main, wq_swap, wuk_pad,
      w_uv.astype(BF16), bf128, freq128, tri)

    n_pairs = MLA_HEADS // 2
    seq_blk = lambda w: pl.BlockSpec((S, w), lambda b, p: (b, p))
    seq_all = lambda w: pl.BlockSpec((S, w), lambda b, p: (b, 0))
    o_a = pl.pallas_call(
        functools.partial(_attn_kernel, fox=False, tq=tq, tk=tk, seq=S),
        grid=(B, n_pairs),
        in_specs=[seq_blk(2 * HEAD_SLOT), seq_blk(2 * HEAD_SLOT),
                  pl.BlockSpec((nkv, LANES, tk), lambda b, p: (b, p, 0))],
        out_specs=seq_blk(LANES),
        out_shape=jax.ShapeDtypeStruct((T, 512), BF16),
        compiler_params=_params(("arbitrary", "arbitrary")),
        name="attn_mla",
    )(q, k, vt)
    o_b = pl.pallas_call(
        functools.partial(_attn_kernel, fox=True, tq=tq, tk=tk, seq=S),
        grid=(B, n_pairs),
        in_specs=[seq_blk(LANES), seq_blk(LANES), seq_all(LANES), seq_all(LANES),
                  pl.BlockSpec((nkv, LANES, tk), lambda b, p: (b, n_pairs + p, 0))],
        out_specs=seq_blk(LANES),
        out_shape=jax.ShapeDtypeStruct((T, 512), BF16),
        compiler_params=_params(("arbitrary", "arbitrary")),
        name="attn_fox",
    )(fq, fk, qaug, kaug, vt)

    n_rt = 64
    wr = jnp.concatenate([w_router.T, w_group.T,
                          jnp.zeros((n_rt - N_EXPERTS - N_GROUPS, D), F32)], axis=0).astype(F32)
    wr_hi = wr.astype(BF16)
    wr_lo = (wr - wr_hi.astype(F32)).astype(BF16)
    br = jnp.concatenate([b_router.astype(F32), b_group.astype(F32),
                          jnp.full((n_rt - N_EXPERTS - N_GROUPS,), NEG_INF, F32)])[:, None]
    h, hn, comb = pl.pallas_call(
        _post_kernel,
        grid=(n_tiles,),
        in_specs=[tok(D), tok(512), tok(512), tok(2 * D), full((512, D)), full((512, D)),
                  full((D, D)), full((1, D)), full((n_rt, D)), full((n_rt, D)), full((n_rt, 1))],
        out_specs=[tok(D), tok(D), tok(LANES)],
        out_shape=[jax.ShapeDtypeStruct((T, D), F32), jax.ShapeDtypeStruct((T, D), BF16),
                   jax.ShapeDtypeStruct((T, LANES), F32)],
        compiler_params=_params(("arbitrary",)),
        name="post",
    )(x2, o_a, o_b, gates, w_o_mla.astype(BF16), w_o_fox.astype(BF16), w_out.astype(BF16),
      row(g_ffn), wr_hi, wr_lo, br)

    tokm = lambda w: pl.BlockSpec((tm_moe, w), lambda i, e: (i, 0))
    out = pl.pallas_call(
        _moe_kernel,
        grid=(T // tm_moe, N_EXPERTS),
        in_specs=[tokm(D), tokm(LANES), tokm(D),
                  pl.BlockSpec((1, D, EXPERT_FF), lambda i, e: (e, 0, 0)),
                  pl.BlockSpec((1, D, EXPERT_FF), lambda i, e: (e, 0, 0)),
                  pl.BlockSpec((1, EXPERT_FF, D), lambda i, e: (e, 0, 0)),
                  pl.BlockSpec((1, D), lambda i, e: (0, 0))],
        out_specs=tokm(D),
        out_shape=jax.ShapeDtypeStruct((T, D), F32),
        scratch_shapes=[pltpu.VMEM((tm_moe, D), F32)],
        compiler_params=_params(("arbitrary", "arbitrary")),
        name="moe",
    )(hn, comb, h, w_e_gate.astype(BF16), w_e_up.astype(BF16), w_e_down.astype(BF16), row(g_final))
    return out.reshape(B, S, D)
```

```python
import functools

import jax
import jax.numpy as jnp
from jax import lax
from jax.experimental import pallas as pl
from jax.experimental.pallas import tpu as pltpu

F32 = jnp.float32
BF16 = jnp.bfloat16

D_MODEL = 1024
CHUNK = 64
EPS = 1e-6
NEG_INF = -1e30
LOG2E = 1.4426950408889634

MLA_HEADS = 8
MLA_NOPE_DIM = 64
MLA_ROPE_DIM = 32
MLA_V_DIM = 64
MLA_Q_RANK = 256
MLA_KV_RANK = 128
ROPE_THETA = 10000.0
FOX_HEADS = 8
FOX_HEAD_DIM = 64
N_GROUPS = 4
EXPERTS_PER_GROUP = 8
N_EXPERTS = 32
EXPERT_FF = 256

LANES = 128
HEAD_SLOT = LANES
AUG_SLOT = 8
ATT_TILE = 256
VMEM_LIMIT_BYTES = 56 * 1024 * 1024

_SEG_WIDTHS = dict(cq=MLA_Q_RANK, ckv=MLA_KV_RANK, krm=HEAD_SLOT, krs=HEAD_SLOT,
                   fq=512, fk=512, fv=512, flog=LANES, ga=D_MODEL, gb=D_MODEL)
_SEG = {}
_off = 0
for _name, _w in _SEG_WIDTHS.items():
    _SEG[_name] = (_off, _off + _w)
    _off += _w
W1_COLS = _off


def _split3(v):
    hi = v.astype(BF16).astype(F32)
    r = v - hi
    mid = r.astype(BF16).astype(F32)
    lo = (r - mid).astype(BF16).astype(F32)
    return hi, mid, lo


def _proj_kernel(x_ref, pos_ref, gmix_ref, w1_ref, gcq_ref, gckv_ref, wqm_ref, wqs_ref, wuk_ref,
                 wuv_ref, bf_ref, freq_ref, tri_ref,
                 q_ref, k_ref, vt_ref, fq_ref, fk_ref, qaug_ref, kaug_ref, gates_ref,
                 carry_ref, *, tiles_per_seq, tk, scale_a, scale_b):
    i = pl.program_id(0)
    tm = x_ref.shape[0]

    x = x_ref[...]
    ms = jnp.mean(x * x, axis=-1, keepdims=True)
    y = ((x * lax.rsqrt(ms + EPS)) * gmix_ref[...]).astype(BF16)

    def seg(name):
        a, b = _SEG[name]
        return jnp.dot(y, w1_ref[:, a:b], preferred_element_type=F32)

    ang = pos_ref[...].astype(F32) * freq_ref[...]
    cos = jnp.cos(ang)
    sin = jnp.sin(ang)

    cq = seg("cq")
    cqn = ((cq * lax.rsqrt(jnp.mean(cq * cq, axis=-1, keepdims=True) + EPS)) * gcq_ref[...]).astype(BF16)
    qm = jnp.dot(cqn, wqm_ref[...], preferred_element_type=F32)
    qs = jnp.dot(cqn, wqs_ref[...], preferred_element_type=F32)
    cos_q = cos * (scale_a * LOG2E)
    sin_q = sin * (scale_a * LOG2E)
    ckv = seg("ckv")
    ckvn = ((ckv * lax.rsqrt(jnp.mean(ckv * ckv, axis=-1, keepdims=True) + EPS)) * gckv_ref[...]).astype(BF16)
    knope = jnp.dot(ckvn, wuk_ref[...], preferred_element_type=F32)
    va = jnp.dot(ckvn, wuv_ref[...], preferred_element_type=F32)
    krope = seg("krm") * cos + seg("krs") * sin
    for h in range(MLA_HEADS):
        sl = slice(h * HEAD_SLOT, (h + 1) * HEAD_SLOT)
        q_ref[:, sl] = (qm[:, sl] * cos_q + qs[:, sl] * sin_q).astype(BF16)
        k_ref[:, sl] = (knope[:, sl] + krope).astype(BF16)

    fq_ref[...] = (seg("fq") * (scale_b * LOG2E)).astype(BF16)
    fk_ref[...] = seg("fk").astype(BF16)
    fv = seg("fv")
    va_t = va.T
    fv_t = fv.T
    for c in range(tm // tk):
        vt_ref[c, 0:512, :] = va_t[:, c * tk:(c + 1) * tk].astype(BF16)
        vt_ref[c, 512:1024, :] = fv_t[:, c * tk:(c + 1) * tk].astype(BF16)

    logf = jax.nn.log_sigmoid(seg("flog") + bf_ref[...])
    hi, mid, lo = _split3(logf)
    tri = tri_ref[...]
    cum = (jnp.dot(tri, hi.astype(BF16), preferred_element_type=F32)
           + jnp.dot(tri, mid.astype(BF16), preferred_element_type=F32)
           + jnp.dot(tri, lo.astype(BF16), preferred_element_type=F32))

    @pl.when(i % tiles_per_seq == 0)
    def _():
        carry_ref[...] = jnp.zeros_like(carry_ref)

    decay = cum + carry_ref[0:1, :]
    carry_ref[0:1, :] = decay[tm - 1:tm, :]
    d_hi, d_mid, d_lo = _split3(decay * LOG2E)
    lane = lax.broadcasted_iota(jnp.int32, (1, LANES), 1)
    l8 = lane % AUG_SLOT
    valid = lane < FOX_HEADS * AUG_SLOT
    one = jnp.ones_like(d_hi)
    zero = jnp.zeros_like(d_hi)
    qaug = jnp.where(l8 == 0, d_hi, jnp.where(l8 == 1, d_mid, jnp.where(l8 == 2, d_lo,
                     jnp.where(l8 < 6, one, zero))))
    kaug = jnp.where(l8 < 3, one, jnp.where(l8 == 3, -d_hi, jnp.where(l8 == 4, -d_mid,
                     jnp.where(l8 == 5, -d_lo, zero))))
    qaug_ref[...] = jnp.where(valid, qaug, zero).astype(BF16)
    kaug_ref[...] = jnp.where(valid, kaug, zero).astype(BF16)

    gates_ref[:, 0:D_MODEL] = jax.nn.sigmoid(seg("ga")).astype(BF16)
    gates_ref[:, D_MODEL:2 * D_MODEL] = jax.nn.sigmoid(seg("gb")).astype(BF16)


def _attn_kernel(*refs, fox, heads, tiles, seq):
    if fox:
        q_ref, k_ref, qaug_ref, kaug_ref, vt_ref, o_ref, m_scr, l_scr, acc_scr = refs
    else:
        q_ref, k_ref, vt_ref, o_ref, m_scr, l_scr, acc_scr = refs
    grp = pl.program_id(1)
    tq = tiles * ATT_TILE
    nq = seq // tq
    lane = lax.broadcasted_iota(jnp.int32, (1, LANES), 1)
    nt = (((1,), (1,)), ((), ()))
    chains = [(h, t) for h in range(heads) for t in range(tiles)]

    def q_block(i, _):
        q0 = pl.multiple_of(i * tq, tq)
        qops = {}
        for h in range(heads):
            for t in range(tiles):
                rows = pl.ds(pl.multiple_of(q0 + t * ATT_TILE, ATT_TILE), ATT_TILE)
                if fox:
                    blk = slice((h // 2) * LANES, (h // 2 + 1) * LANES)
                    qp = q_ref[rows, blk]
                    qa = qaug_ref[rows, :]
                    zq = jnp.zeros_like(qp)
                    qh = jnp.where(lane // FOX_HEAD_DIM == h % 2, qp, zq)
                    qah = jnp.where(lane // AUG_SLOT == heads * grp + h, qa, zq)
                    qops[(h, t)] = jnp.concatenate([qh, qah], axis=1)
                else:
                    qops[(h, t)] = q_ref[rows, h * HEAD_SLOT:(h + 1) * HEAD_SLOT]
        for c in range(len(chains)):
            m_scr[c] = jnp.full((8, ATT_TILE), NEG_INF, F32)
            l_scr[c] = jnp.zeros((8, ATT_TILE), F32)
            acc_scr[c] = jnp.zeros((64, ATT_TILE), F32)

        def kv_step(j, diag):
            k0 = pl.multiple_of(j * ATT_TILE, ATT_TILE)
            krows = pl.ds(k0, ATT_TILE)
            active = [(c, ht) for c, ht in enumerate(chains) if diag is None or ht[1] >= diag]
            kops = {}
            for h in range(heads):
                if fox:
                    if h % 2 == 0:
                        blk = slice((h // 2) * LANES, (h // 2 + 1) * LANES)
                        kops[h] = jnp.concatenate([k_ref[krows, blk], kaug_ref[krows, :]], axis=1)
                    else:
                        kops[h] = kops[h - 1]
                else:
                    kops[h] = k_ref[krows, h * HEAD_SLOT:(h + 1) * HEAD_SLOT]
            s_all = {c: lax.dot_general(kops[ht[0]], qops[ht], nt, preferred_element_type=F32)
                     for c, ht in active}
            if diag is not None:
                kpos = lax.broadcasted_iota(jnp.int32, (ATT_TILE, 1), 0)
                qpos = lax.broadcasted_iota(jnp.int32, (1, ATT_TILE), 1)
                allowed = (kpos <= qpos) if fox else ((kpos // CHUNK) <= (qpos // CHUNK))
            p_all, alpha_all = {}, {}
            for c, ht in active:
                s = s_all[c]
                if diag is not None and ht[1] == diag:
                    s = jnp.where(allowed, s, NEG_INF)
                m = m_scr[c][0:1, :]
                m_new = jnp.maximum(m, jnp.max(s, axis=0, keepdims=True))
                alpha = jnp.exp2(m - m_new)
                p = jnp.exp2(s - m_new)
                l_scr[c] = jnp.broadcast_to(alpha * l_scr[c][0:1, :] + jnp.sum(p, axis=0, keepdims=True),
                                            (8, ATT_TILE))
                m_scr[c] = jnp.broadcast_to(m_new, (8, ATT_TILE))
                p_all[c] = p.astype(BF16)
                alpha_all[c] = alpha
            for c, ht in active:
                vt = vt_ref[j, ht[0] * 64:(ht[0] + 1) * 64, :]
                pv = jnp.dot(vt, p_all[c], preferred_element_type=F32)
                acc_scr[c] = alpha_all[c] * acc_scr[c] + pv

        def full_step(j, _):
            kv_step(j, None)
            return 0

        lax.fori_loop(0, i * tiles, full_step, 0)
        for d in range(tiles):
            kv_step(i * tiles + d, d)
        for t in range(tiles):
            rows = pl.ds(pl.multiple_of(q0 + t * ATT_TILE, ATT_TILE), ATT_TILE)
            for hp in range(heads // 2):
                c0 = chains.index((2 * hp, t))
                c1 = chains.index((2 * hp + 1, t))
                o_t = jnp.concatenate([acc_scr[c0] / l_scr[c0][0:1, :],
                                       acc_scr[c1] / l_scr[c1][0:1, :]], axis=0)
                o_ref[rows, hp * LANES:(hp + 1) * LANES] = o_t.T.astype(BF16)
        return 0

    lax.fori_loop(0, nq, q_block, 0)


def _post_kernel(x_ref, oa_ref, ob_ref, gates_ref, woa_ref, wob_ref, wout_ref, gffn_ref,
                 wrh_ref, wrl_ref, br_ref, h_ref, hn_ref, comb_ref):
    tm = x_ref.shape[0]
    ya = jnp.dot(oa_ref[...], woa_ref[...], preferred_element_type=F32)
    yb = jnp.dot(ob_ref[...], wob_ref[...], preferred_element_type=F32)
    sa = gates_ref[:, 0:D_MODEL].astype(F32)
    sb = gates_ref[:, D_MODEL:2 * D_MODEL].astype(F32)
    mixed = (sa * ya + sb * yb).astype(BF16)
    h = x_ref[...] + jnp.dot(mixed, wout_ref[...], preferred_element_type=F32)
    h_ref[...] = h
    hn = (h * lax.rsqrt(jnp.mean(h * h, axis=-1, keepdims=True) + EPS)) * gffn_ref[...]
    hn_hi = hn.astype(BF16)
    hn_ref[...] = hn_hi
    hn_lo = (hn - hn_hi.astype(F32)).astype(BF16)

    nt = (((1,), (1,)), ((), ()))
    wrh = wrh_ref[...]
    lt = (lax.dot_general(wrh, hn_hi, nt, preferred_element_type=F32)
          + lax.dot_general(wrh, hn_lo, nt, preferred_element_type=F32)
          + lax.dot_general(wrl_ref[...], hn_hi, nt, preferred_element_type=F32)) + br_ref[...]

    row8 = lax.broadcasted_iota(jnp.int32, (EXPERTS_PER_GROUP, tm), 0).astype(F32)
    big = jnp.float32(EXPERTS_PER_GROUP)
    lg = lt[N_EXPERTS:N_EXPERTS + 8, :]
    gmax = jnp.max(lg, axis=0, keepdims=True)
    g_p = 1.0 / jnp.sum(jnp.exp(lg - gmax), axis=0, keepdims=True)
    g_idx = jnp.min(jnp.where(lg == gmax, row8, big), axis=0, keepdims=True)
    sel = jnp.zeros((EXPERTS_PER_GROUP, tm), F32)
    for g in range(N_GROUPS):
        sel = jnp.where(g_idx == g, lt[g * EXPERTS_PER_GROUP:(g + 1) * EXPERTS_PER_GROUP, :], sel)
    m1 = jnp.max(sel, axis=0, keepdims=True)
    i1 = jnp.min(jnp.where(sel == m1, row8, big), axis=0, keepdims=True)
    sel2 = jnp.where(row8 == i1, -jnp.inf, sel)
    m2 = jnp.max(sel2, axis=0, keepdims=True)
    i2 = jnp.min(jnp.where(sel2 == m2, row8, big), axis=0, keepdims=True)
    r = jnp.exp(m2 - m1)
    w1 = g_p / (1.0 + r)
    w2 = g_p * r / (1.0 + r)
    e1 = g_idx * EXPERTS_PER_GROUP + i1
    e2 = g_idx * EXPERTS_PER_GROUP + i2
    row = lax.broadcasted_iota(jnp.int32, (LANES, tm), 0).astype(F32)
    comb_t = jnp.where(row == e1, w1, 0.0) + jnp.where(row == e2, w2, 0.0)
    comb_ref[...] = comb_t.T


def _moe_kernel(hn_ref, comb_ref, h_ref, wg_ref, wu_ref, wd_ref, gfin_ref, out_ref, acc_ref):
    e = pl.program_id(1)

    @pl.when(e == 0)
    def _():
        acc_ref[...] = jnp.zeros_like(acc_ref)

    hn = hn_ref[...]
    g = jnp.dot(hn, wg_ref[0], preferred_element_type=F32)
    u = jnp.dot(hn, wu_ref[0], preferred_element_type=F32)
    a = (jax.nn.silu(g) * u).astype(BF16)
    lane = lax.broadcasted_iota(jnp.int32, (1, LANES), 1)
    c = jnp.sum(jnp.where(lane == e, comb_ref[...], 0.0), axis=-1, keepdims=True)
    acc_ref[...] += c * jnp.dot(a, wd_ref[0], preferred_element_type=F32)

    @pl.when(e == N_EXPERTS - 1)
    def _():
        hh = h_ref[...] + acc_ref[...]
        out_ref[...] = (hh * lax.rsqrt(jnp.mean(hh * hh, axis=-1, keepdims=True) + EPS)) * gfin_ref[...]


def _pad_heads(w, heads, dim):
    k = w.shape[0]
    w = w.reshape(k, heads, dim)
    return jnp.pad(w, ((0, 0), (0, 0), (0, HEAD_SLOT - dim))).reshape(k, heads * HEAD_SLOT)


def _swap_rope(w_rope):
    half = MLA_ROPE_DIM // 2
    return jnp.concatenate([-w_rope[..., half:], w_rope[..., :half]], axis=-1)


def _params(sem):
    return pltpu.CompilerParams(dimension_semantics=sem, vmem_limit_bytes=VMEM_LIMIT_BYTES)


def kernel(x, positions, g_mix, w_in, b_forget, g_cq, g_ckv, w_uq, w_uk, w_uv, w_o_mla, w_o_fox,
           w_out, g_ffn, w_group, b_group, w_router, b_router, w_e_gate, w_e_up, w_e_down, g_final):
    B, S, D = x.shape
    T = B * S
    assert D == D_MODEL
    tm = min(512, S)
    tk = ATT_TILE
    att_heads = 4
    att_tiles = 2 if S % (2 * ATT_TILE) == 0 else 1
    tm_moe = min(1024, S)
    assert S % tm == 0 and S % (att_tiles * ATT_TILE) == 0 and tm % tk == 0 and tk % CHUNK == 0

    cq_w, ckv_w, kr_w, fq_w, fk_w, fv_w, fl_w, ga_w, gb_w = jnp.split(
        w_in, [256, 384, 416, 928, 1440, 1952, 1960, 2984], axis=1)
    z64 = jnp.zeros((D, MLA_NOPE_DIM), F32)
    z32 = jnp.zeros((D, HEAD_SLOT - MLA_NOPE_DIM - MLA_ROPE_DIM), F32)
    fl_rep = jnp.pad(jnp.repeat(fl_w, AUG_SLOT, axis=1), ((0, 0), (0, LANES - FOX_HEADS * AUG_SLOT)))
    w1 = jnp.concatenate([cq_w, ckv_w,
                          jnp.concatenate([z64, kr_w, z32], axis=1),
                          jnp.concatenate([z64, _swap_rope(kr_w), z32], axis=1),
                          fq_w, fk_w, fv_w, fl_rep, ga_w, gb_w], axis=1).astype(BF16)
    assert w1.shape[1] == W1_COLS
    bf128 = jnp.pad(jnp.repeat(b_forget.astype(F32), AUG_SLOT), (0, LANES - FOX_HEADS * AUG_SLOT))[None, :]
    dq = MLA_NOPE_DIM + MLA_ROPE_DIM
    wq3 = w_uq.reshape(MLA_Q_RANK, MLA_HEADS, dq)
    wq_main = _pad_heads(w_uq, MLA_HEADS, dq).astype(BF16)
    wq_swap3 = jnp.concatenate([jnp.zeros((MLA_Q_RANK, MLA_HEADS, MLA_NOPE_DIM), F32),
                                _swap_rope(wq3[..., MLA_NOPE_DIM:])], axis=-1)
    wq_swap = _pad_heads(wq_swap3.reshape(MLA_Q_RANK, MLA_HEADS * dq), MLA_HEADS, dq).astype(BF16)
    wuk_pad = _pad_heads(w_uk, MLA_HEADS, MLA_NOPE_DIM).astype(BF16)
    half = MLA_ROPE_DIM // 2
    inv_freq = ROPE_THETA ** (-jnp.arange(half, dtype=F32) / half)
    freq128 = jnp.concatenate([jnp.zeros((MLA_NOPE_DIM,), F32), inv_freq, inv_freq,
                               jnp.zeros((HEAD_SLOT - dq,), F32)])[None, :]
    tri = (jnp.arange(tm)[:, None] >= jnp.arange(tm)[None, :]).astype(BF16)

    x2 = x.reshape(T, D)
    pos2 = positions.reshape(T, 1).astype(jnp.int32)
    row = lambda v: v.astype(F32)[None, :]

    full = lambda shape: pl.BlockSpec(shape, lambda i: (0,) * len(shape))
    tok = lambda w: pl.BlockSpec((tm, w), lambda i: (i, 0))
    n_tiles = T // tm
    nkv = S // tk

    q, k, vt, fq, fk, qaug, kaug, gates = pl.pallas_call(
        functools.partial(_proj_kernel, tiles_per_seq=S // tm, tk=tk,
                          scale_a=float(dq) ** -0.5, scale_b=float(FOX_HEAD_DIM) ** -0.5),
        grid=(n_tiles,),
        in_specs=[tok(D), tok(1), full((1, D)), full((D, W1_COLS)), full((1, MLA_Q_RANK)),
                  full((1, MLA_KV_RANK)), full((MLA_Q_RANK, MLA_HEADS * HEAD_SLOT)),
                  full((MLA_Q_RANK, MLA_HEADS * HEAD_SLOT)), full((MLA_KV_RANK, MLA_HEADS * HEAD_SLOT)),
                  full((MLA_KV_RANK, 512)), full((1, LANES)), full((1, LANES)), full((tm, tm))],
        out_specs=[tok(1024), tok(1024),
                   pl.BlockSpec((tm // tk, 1024, tk), lambda i: (i, 0, 0)),
                   tok(512), tok(512), tok(LANES), tok(LANES), tok(2 * D)],
        out_shape=[jax.ShapeDtypeStruct((T, 1024), BF16), jax.ShapeDtypeStruct((T, 1024), BF16),
                   jax.ShapeDtypeStruct((T // tk, 1024, tk), BF16),
                   jax.ShapeDtypeStruct((T, 512), BF16), jax.ShapeDtypeStruct((T, 512), BF16),
                   jax.ShapeDtypeStruct((T, LANES), BF16), jax.ShapeDtypeStruct((T, LANES), BF16),
                   jax.ShapeDtypeStruct((T, 2 * D), BF16)],
        scratch_shapes=[pltpu.VMEM((8, LANES), F32)],
        compiler_params=_params(("arbitrary",)),
        name="proj",
    )(x2, pos2, row(g_mix), w1, row(g_cq), row(g_ckv), wq_main, wq_swap, wuk_pad,
      w_uv.astype(BF16), bf128, freq128, tri)

    n_grp = MLA_HEADS // att_heads
    n_chains = att_heads * att_tiles
    seq_blk = lambda w: pl.BlockSpec((S, w), lambda b, g: (b, g))
    seq_all = lambda w: pl.BlockSpec((S, w), lambda b, g: (b, 0))
    att_scratch = [pltpu.VMEM((n_chains, 8, ATT_TILE), F32), pltpu.VMEM((n_chains, 8, ATT_TILE), F32),
                   pltpu.VMEM((n_chains, 64, ATT_TILE), F32)]
    o_a = pl.pallas_call(
        functools.partial(_attn_kernel, fox=False, heads=att_heads, tiles=att_tiles, seq=S),
        grid=(B, n_grp),
        in_specs=[seq_blk(att_heads * HEAD_SLOT), seq_blk(att_heads * HEAD_SLOT),
                  pl.BlockSpec((nkv, att_heads * 64, tk), lambda b, g: (b, g, 0))],
        out_specs=seq_blk(att_heads * 64),
        out_shape=jax.ShapeDtypeStruct((T, 512), BF16),
        scratch_shapes=att_scratch,
        compiler_params=_params(("arbitrary", "arbitrary")),
        name="attn_mla",
    )(q, k, vt)
    o_b = pl.pallas_call(
        functools.partial(_attn_kernel, fox=True, heads=att_heads, tiles=att_tiles, seq=S),
        grid=(B, n_grp),
        in_specs=[seq_blk(att_heads * 64), seq_blk(att_heads * 64), seq_all(LANES), seq_all(LANES),
                  pl.BlockSpec((nkv, att_heads * 64, tk), lambda b, g: (b, n_grp + g, 0))],
        out_specs=seq_blk(att_heads * 64),
        out_shape=jax.ShapeDtypeStruct((T, 512), BF16),
        scratch_shapes=att_scratch,
        compiler_params=_params(("arbitrary", "arbitrary")),
        name="attn_fox",
    )(fq, fk, qaug, kaug, vt)

    n_rt = 64
    wr = jnp.concatenate([w_router.T, w_group.T,
                          jnp.zeros((n_rt - N_EXPERTS - N_GROUPS, D), F32)], axis=0).astype(F32)
    wr_hi = wr.astype(BF16)
    wr_lo = (wr - wr_hi.astype(F32)).astype(BF16)
    br = jnp.concatenate([b_router.astype(F32), b_group.astype(F32),
                          jnp.full((n_rt - N_EXPERTS - N_GROUPS,), NEG_INF, F32)])[:, None]
    h, hn, comb = pl.pallas_call(
        _post_kernel,
        grid=(n_tiles,),
        in_specs=[tok(D), tok(512), tok(512), tok(2 * D), full((512, D)), full((512, D)),
                  full((D, D)), full((1, D)), full((n_rt, D)), full((n_rt, D)), full((n_rt, 1))],
        out_specs=[tok(D), tok(D), tok(LANES)],
        out_shape=[jax.ShapeDtypeStruct((T, D), F32), jax.ShapeDtypeStruct((T, D), BF16),
                   jax.ShapeDtypeStruct((T, LANES), F32)],
        compiler_params=_params(("arbitrary",)),
        name="post",
    )(x2, o_a, o_b, gates, w_o_mla.astype(BF16), w_o_fox.astype(BF16), w_out.astype(BF16),
      row(g_ffn), wr_hi, wr_lo, br)

    tokm = lambda w: pl.BlockSpec((tm_moe, w), lambda i, e: (i, 0))
    out = pl.pallas_call(
        _moe_kernel,
        grid=(T // tm_moe, N_EXPERTS),
        in_specs=[tokm(D), tokm(LANES), tokm(D),
                  pl.BlockSpec((1, D, EXPERT_FF), lambda i, e: (e, 0, 0)),
                  pl.BlockSpec((1, D, EXPERT_FF), lambda i, e: (e, 0, 0)),
                  pl.BlockSpec((1, EXPERT_FF, D), lambda i, e: (e, 0, 0)),
                  pl.BlockSpec((1, D), lambda i, e: (0, 0))],
        out_specs=tokm(D),
        out_shape=jax.ShapeDtypeStruct((T, D), F32),
        scratch_shapes=[pltpu.VMEM((tm_moe, D), F32)],
        compiler_params=_params(("arbitrary", "arbitrary")),
        name="moe",
    )(hn, comb, h, w_e_gate.astype(BF16), w_e_up.astype(BF16), w_e_down.astype(BF16), row(g_final))
    return out.reshape(B, S, D)
```

```python
import functools

import jax
import jax.numpy as jnp
from jax import lax
from jax.experimental import pallas as pl
from jax.experimental.pallas import tpu as pltpu

F32 = jnp.float32
BF16 = jnp.bfloat16

D_MODEL = 1024
CHUNK = 64
EPS = 1e-6
NEG_INF = -1e30
LOG2E = 1.4426950408889634

MLA_HEADS = 8
MLA_NOPE_DIM = 64
MLA_ROPE_DIM = 32
MLA_V_DIM = 64
MLA_Q_RANK = 256
MLA_KV_RANK = 128
ROPE_THETA = 10000.0
FOX_HEADS = 8
FOX_HEAD_DIM = 64
N_GROUPS = 4
EXPERTS_PER_GROUP = 8
N_EXPERTS = 32
EXPERT_FF = 256

LANES = 128
HEAD_SLOT = LANES
AUG_SLOT = 8
ATT_TILE = 256
VMEM_LIMIT_BYTES = 56 * 1024 * 1024

_SEG_WIDTHS = dict(cq=MLA_Q_RANK, ckv=MLA_KV_RANK, krm=HEAD_SLOT, krs=HEAD_SLOT,
                   fq=512, fk=512, fv=512, flog=LANES, ga=D_MODEL, gb=D_MODEL)
_SEG = {}
_off = 0
for _name, _w in _SEG_WIDTHS.items():
    _SEG[_name] = (_off, _off + _w)
    _off += _w
W1_COLS = _off


def _split3(v):
    hi = v.astype(BF16).astype(F32)
    r = v - hi
    mid = r.astype(BF16).astype(F32)
    lo = (r - mid).astype(BF16).astype(F32)
    return hi, mid, lo


def _proj_kernel(x_ref, pos_ref, gmix_ref, w1_ref, gcq_ref, gckv_ref, wqm_ref, wqs_ref, wuk_ref,
                 wuv_ref, bf_ref, freq_ref, tri_ref,
                 q_ref, k_ref, vt_ref, fq_ref, fk_ref, qaug_ref, kaug_ref, gates_ref,
                 carry_ref, *, tiles_per_seq, tk, scale_a, scale_b):
    i = pl.program_id(0)
    tm = x_ref.shape[0]

    x = x_ref[...]
    ms = jnp.mean(x * x, axis=-1, keepdims=True)
    y = ((x * lax.rsqrt(ms + EPS)) * gmix_ref[...]).astype(BF16)

    def seg(name):
        a, b = _SEG[name]
        return jnp.dot(y, w1_ref[:, a:b], preferred_element_type=F32)

    ang = pos_ref[...].astype(F32) * freq_ref[...]
    cos = jnp.cos(ang)
    sin = jnp.sin(ang)

    cq = seg("cq")
    cqn = ((cq * lax.rsqrt(jnp.mean(cq * cq, axis=-1, keepdims=True) + EPS)) * gcq_ref[...]).astype(BF16)
    qm = jnp.dot(cqn, wqm_ref[...], preferred_element_type=F32)
    qs = jnp.dot(cqn, wqs_ref[...], preferred_element_type=F32)
    cos_q = cos * (scale_a * LOG2E)
    sin_q = sin * (scale_a * LOG2E)
    ckv = seg("ckv")
    ckvn = ((ckv * lax.rsqrt(jnp.mean(ckv * ckv, axis=-1, keepdims=True) + EPS)) * gckv_ref[...]).astype(BF16)
    knope = jnp.dot(ckvn, wuk_ref[...], preferred_element_type=F32)
    va = jnp.dot(ckvn, wuv_ref[...], preferred_element_type=F32)
    krope = seg("krm") * cos + seg("krs") * sin
    for h in range(MLA_HEADS):
        sl = slice(h * HEAD_SLOT, (h + 1) * HEAD_SLOT)
        q_ref[:, sl] = (qm[:, sl] * cos_q + qs[:, sl] * sin_q).astype(BF16)
        k_ref[:, sl] = (knope[:, sl] + krope).astype(BF16)

    fq_ref[...] = (seg("fq") * (scale_b * LOG2E)).astype(BF16)
    fk_ref[...] = seg("fk").astype(BF16)
    fv = seg("fv")
    va_t = va.T
    fv_t = fv.T
    for c in range(tm // tk):
        vt_ref[c, 0:512, :] = va_t[:, c * tk:(c + 1) * tk].astype(BF16)
        vt_ref[c, 512:1024, :] = fv_t[:, c * tk:(c + 1) * tk].astype(BF16)

    logf = jax.nn.log_sigmoid(seg("flog") + bf_ref[...])
    hi, mid, lo = _split3(logf)
    tri = tri_ref[...]
    cum = (jnp.dot(tri, hi.astype(BF16), preferred_element_type=F32)
           + jnp.dot(tri, mid.astype(BF16), preferred_element_type=F32)
           + jnp.dot(tri, lo.astype(BF16), preferred_element_type=F32))

    @pl.when(i % tiles_per_seq == 0)
    def _():
        carry_ref[...] = jnp.zeros_like(carry_ref)

    decay = cum + carry_ref[0:1, :]
    carry_ref[0:1, :] = decay[tm - 1:tm, :]
    d_hi, d_mid, d_lo = _split3(decay * LOG2E)
    lane = lax.broadcasted_iota(jnp.int32, (1, LANES), 1)
    l8 = lane % AUG_SLOT
    valid = lane < FOX_HEADS * AUG_SLOT
    one = jnp.ones_like(d_hi)
    zero = jnp.zeros_like(d_hi)
    qaug = jnp.where(l8 == 0, d_hi, jnp.where(l8 == 1, d_mid, jnp.where(l8 == 2, d_lo,
                     jnp.where(l8 < 6, one, zero))))
    kaug = jnp.where(l8 < 3, one, jnp.where(l8 == 3, -d_hi, jnp.where(l8 == 4, -d_mid,
                     jnp.where(l8 == 5, -d_lo, zero))))
    qaug_ref[...] = jnp.where(valid, qaug, zero).astype(BF16)
    kaug_ref[...] = jnp.where(valid, kaug, zero).astype(BF16)

    gates_ref[:, 0:D_MODEL] = jax.nn.sigmoid(seg("ga")).astype(BF16)
    gates_ref[:, D_MODEL:2 * D_MODEL] = jax.nn.sigmoid(seg("gb")).astype(BF16)


def _attn_kernel(*refs, fox, heads, tiles, seq):
    if fox:
        q_ref, k_ref, qaug_ref, kaug_ref, vt_ref, o_ref, m_scr, l_scr, acc_scr = refs
    else:
        q_ref, k_ref, vt_ref, o_ref, m_scr, l_scr, acc_scr = refs
    grp = pl.program_id(1)
    tq = tiles * ATT_TILE
    nq = seq // tq
    lane = lax.broadcasted_iota(jnp.int32, (1, LANES), 1)
    nt = (((1,), (1,)), ((), ()))
    chains = [(h, t) for h in range(heads) for t in range(tiles)]

    def q_block(i, _):
        q0 = pl.multiple_of(i * tq, tq)
        qops = {}
        for h in range(heads):
            for t in range(tiles):
                rows = pl.ds(pl.multiple_of(q0 + t * ATT_TILE, ATT_TILE), ATT_TILE)
                if fox:
                    blk = slice((h // 2) * LANES, (h // 2 + 1) * LANES)
                    qp = q_ref[rows, blk]
                    qa = qaug_ref[rows, :]
                    zq = jnp.zeros_like(qp)
                    qh = jnp.where(lane // FOX_HEAD_DIM == h % 2, qp, zq)
                    qah = jnp.where(lane // AUG_SLOT == heads * grp + h, qa, zq)
                    qops[(h, t)] = jnp.concatenate([qh, qah], axis=1)
                else:
                    qops[(h, t)] = q_ref[rows, h * HEAD_SLOT:(h + 1) * HEAD_SLOT]
        for c in range(len(chains)):
            m_scr[c] = jnp.full((8, ATT_TILE), NEG_INF, F32)
            l_scr[c] = jnp.zeros((8, ATT_TILE), F32)
            acc_scr[c] = jnp.zeros((64, ATT_TILE), F32)

        def kv_step(j, diag):
            k0 = pl.multiple_of(j * ATT_TILE, ATT_TILE)
            krows = pl.ds(k0, ATT_TILE)
            active = [(c, ht) for c, ht in enumerate(chains) if diag is None or ht[1] >= diag]
            kops = {}
            for h in range(heads):
                if fox:
                    if h % 2 == 0:
                        blk = slice((h // 2) * LANES, (h // 2 + 1) * LANES)
                        kops[h] = jnp.concatenate([k_ref[krows, blk], kaug_ref[krows, :]], axis=1)
                    else:
                        kops[h] = kops[h - 1]
                else:
                    kops[h] = k_ref[krows, h * HEAD_SLOT:(h + 1) * HEAD_SLOT]
            s_all = {c: lax.dot_general(kops[ht[0]], qops[ht], nt, preferred_element_type=F32)
                     for c, ht in active}
            if diag is not None:
                kpos = lax.broadcasted_iota(jnp.int32, (ATT_TILE, 1), 0)
                qpos = lax.broadcasted_iota(jnp.int32, (1, ATT_TILE), 1)
                allowed = (kpos <= qpos) if fox else ((kpos // CHUNK) <= (qpos // CHUNK))
            p_all, alpha_all = {}, {}
            for c, ht in active:
                s = s_all[c]
                if diag is not None and ht[1] == diag:
                    s = jnp.where(allowed, s, NEG_INF)
                m = m_scr[c][0:1, :]
                m_new = jnp.maximum(m, jnp.max(s, axis=0, keepdims=True))
                alpha = jnp.exp2(m - m_new)
                p = jnp.exp2(s - m_new)
                l_scr[c] = jnp.broadcast_to(alpha * l_scr[c][0:1, :] + jnp.sum(p, axis=0, keepdims=True),
                                            (8, ATT_TILE))
                m_scr[c] = jnp.broadcast_to(m_new, (8, ATT_TILE))
                p_all[c] = p.astype(BF16)
                alpha_all[c] = alpha
            for c, ht in active:
                vt = vt_ref[j, ht[0] * 64:(ht[0] + 1) * 64, :]
                pv = jnp.dot(vt, p_all[c], preferred_element_type=F32)
                acc_scr[c] = alpha_all[c] * acc_scr[c] + pv

        def full_step(j, _):
            kv_step(j, None)
            return 0

        lax.fori_loop(0, i * tiles, full_step, 0)
        for d in range(tiles):
            kv_step(i * tiles + d, d)
        for t in range(tiles):
            rows = pl.ds(pl.multiple_of(q0 + t * ATT_TILE, ATT_TILE), ATT_TILE)
            for hp in range(heads // 2):
                c0 = chains.index((2 * hp, t))
                c1 = chains.index((2 * hp + 1, t))
                o_t = jnp.concatenate([acc_scr[c0] / l_scr[c0][0:1, :],
                                       acc_scr[c1] / l_scr[c1][0:1, :]], axis=0)
                o_ref[rows, hp * LANES:(hp + 1) * LANES] = o_t.T.astype(BF16)
        return 0

    lax.fori_loop(0, nq, q_block, 0)


def _post_kernel(x_ref, oa_ref, ob_ref, gates_ref, woa_ref, wob_ref, wout_ref, gffn_ref,
                 wrh_ref, wrl_ref, br_ref, upper_ref, h_ref, hnp_ref, meta_ref, wts_ref, cnt_ref, cnt_scr):
    tm = x_ref.shape[0]
    ya = jnp.dot(oa_ref[...], woa_ref[...], preferred_element_type=F32)
    yb = jnp.dot(ob_ref[...], wob_ref[...], preferred_element_type=F32)
    sa = gates_ref[:, 0:D_MODEL].astype(F32)
    sb = gates_ref[:, D_MODEL:2 * D_MODEL].astype(F32)
    mixed = (sa * ya + sb * yb).astype(BF16)
    h = x_ref[...] + jnp.dot(mixed, wout_ref[...], preferred_element_type=F32)
    h_ref[...] = h
    hn = (h * lax.rsqrt(jnp.mean(h * h, axis=-1, keepdims=True) + EPS)) * gffn_ref[...]
    hn_hi = hn.astype(BF16)
    hn_hi32 = hn_hi.astype(F32)
    hn_lo = (hn - hn_hi32).astype(BF16)
    half = D_MODEL // 2
    bits = pltpu.bitcast(hn_hi32, jnp.uint32)
    hnp_ref[...] = (bits[:, 0:half] >> 16) | (bits[:, half:D_MODEL] & jnp.uint32(0xFFFF0000))

    nt = (((1,), (1,)), ((), ()))
    wrh = wrh_ref[...]
    lt = (lax.dot_general(wrh, hn_hi, nt, preferred_element_type=F32)
          + lax.dot_general(wrh, hn_lo, nt, preferred_element_type=F32)
          + lax.dot_general(wrl_ref[...], hn_hi, nt, preferred_element_type=F32)) + br_ref[...]

    row8 = lax.broadcasted_iota(jnp.int32, (EXPERTS_PER_GROUP, tm), 0).astype(F32)
    big = jnp.float32(EXPERTS_PER_GROUP)
    lg = lt[N_EXPERTS:N_EXPERTS + 8, :]
    gmax = jnp.max(lg, axis=0, keepdims=True)
    g_p = 1.0 / jnp.sum(jnp.exp(lg - gmax), axis=0, keepdims=True)
    g_idx = jnp.min(jnp.where(lg == gmax, row8, big), axis=0, keepdims=True)
    sel = jnp.zeros((EXPERTS_PER_GROUP, tm), F32)
    for g in range(N_GROUPS):
        sel = jnp.where(g_idx == g, lt[g * EXPERTS_PER_GROUP:(g + 1) * EXPERTS_PER_GROUP, :], sel)
    m1 = jnp.max(sel, axis=0, keepdims=True)
    i1 = jnp.min(jnp.where(sel == m1, row8, big), axis=0, keepdims=True)
    sel2 = jnp.where(row8 == i1, -jnp.inf, sel)
    m2 = jnp.max(sel2, axis=0, keepdims=True)
    i2 = jnp.min(jnp.where(sel2 == m2, row8, big), axis=0, keepdims=True)
    r = jnp.exp(m2 - m1)
    w1 = g_p / (1.0 + r)
    w2 = g_p * r / (1.0 + r)
    e1 = g_idx * EXPERTS_PER_GROUP + i1
    e2 = g_idx * EXPERTS_PER_GROUP + i2
    @pl.when(pl.program_id(0) == 0)
    def _():
        cnt_scr[...] = jnp.zeros_like(cnt_scr)

    rowe = lax.broadcasted_iota(jnp.int32, (N_EXPERTS, tm), 0).astype(F32)
    is1 = rowe == e1
    is2 = rowe == e2
    onehot = jnp.where(is1 | is2, 1.0, 0.0)
    before = jnp.dot(onehot.astype(BF16), upper_ref[...], preferred_element_type=F32) + cnt_scr[:, 0:1]
    r1 = jnp.sum(jnp.where(is1, before, 0.0), axis=0, keepdims=True)
    r2 = jnp.sum(jnp.where(is2, before, 0.0), axis=0, keepdims=True)
    cnt_scr[...] = cnt_scr[...] + jnp.sum(onehot, axis=1, keepdims=True)
    cnt_ref[...] = cnt_scr[...]
    zrow = jnp.zeros_like(e1)
    meta_ref[0] = jnp.concatenate([e1, e2, r1, r2, zrow, zrow, zrow, zrow], axis=0).astype(jnp.int32)
    row = lax.broadcasted_iota(jnp.int32, (LANES, tm), 0)
    wts_t = jnp.where(row == 0, w1, jnp.where(row == 1, w2, 0.0))
    wts_ref[...] = wts_t.T


def _dispatch_kernel(off_ref, cnt_ref, nused_ref, meta_hbm, hnp_hbm, xs_hbm, meta_smem, zbuf, sem_meta,
                     sem_zero, sem_rows, *, tm, tm_e, n_tiles_e):
    i = pl.program_id(0)
    meta_cp = pltpu.make_async_copy(meta_hbm.at[i], meta_smem, sem_meta)
    meta_cp.start()

    @pl.when(i == 0)
    def _():
        zbuf[...] = jnp.zeros_like(zbuf)
        shift = tm_e.bit_length() - 1

        def unused_tile(t, _):
            cp = pltpu.make_async_copy(zbuf, xs_hbm.at[pl.ds(pl.multiple_of(t * tm_e, tm_e), tm_e)], sem_zero)
            cp.start()
            cp.wait()
            return 0

        lax.fori_loop(nused_ref[0], n_tiles_e, unused_tile, 0)

        def tail_copy(e):
            n = cnt_ref[e]
            start = off_ref[e] + lax.shift_left(lax.shift_right_logical(jnp.maximum(n, 1) - 1, shift), shift)
            return n, pltpu.make_async_copy(zbuf, xs_hbm.at[pl.ds(pl.multiple_of(start, tm_e), tm_e)], sem_zero)

        for e in range(N_EXPERTS):
            n, cp = tail_copy(e)

            @pl.when(n > 0)
            def _():
                cp.start()
        for e in range(N_EXPERTS):
            n, cp = tail_copy(e)

            @pl.when(n > 0)
            def _():
                cp.wait()

    meta_cp.wait()
    base = i * tm

    def row_copies(t):
        src = hnp_hbm.at[pl.ds(base + t, 1)]
        p1 = off_ref[meta_smem[0, t]] + meta_smem[2, t]
        p2 = off_ref[meta_smem[1, t]] + meta_smem[3, t]
        return (pltpu.make_async_copy(src, xs_hbm.at[pl.ds(p1, 1)], sem_rows),
                pltpu.make_async_copy(src, xs_hbm.at[pl.ds(p2, 1)], sem_rows))

    def issue(t, _):
        for cp in row_copies(t):
            cp.start()
        return 0

    def drain(t, _):
        for cp in row_copies(t):
            cp.wait()
        return 0

    lax.fori_loop(0, tm, issue, 0, unroll=8)
    lax.fori_loop(0, tm, drain, 0, unroll=8)


def _moe_kernel(texp_ref, nused_ref, xs_ref, wg_ref, wu_ref, wd_ref, ys_ref):
    del texp_ref
    half = D_MODEL // 2

    @pl.when(pl.program_id(0) >= nused_ref[0])
    def _():
        ys_ref[...] = jnp.zeros_like(ys_ref)

    @pl.when(pl.program_id(0) < nused_ref[0])
    def _():
        u32 = xs_ref[...]
        x_lo = pltpu.bitcast(u32 << 16, F32).astype(BF16)
        x_hi = pltpu.bitcast(u32 & jnp.uint32(0xFFFF0000), F32).astype(BF16)
        g = (jnp.dot(x_lo, wg_ref[0, 0:half, :], preferred_element_type=F32)
             + jnp.dot(x_hi, wg_ref[0, half:D_MODEL, :], preferred_element_type=F32))
        u = (jnp.dot(x_lo, wu_ref[0, 0:half, :], preferred_element_type=F32)
             + jnp.dot(x_hi, wu_ref[0, half:D_MODEL, :], preferred_element_type=F32))
        a = (jax.nn.silu(g) * u).astype(BF16)
        ys_ref[...] = jnp.dot(a, wd_ref[0], preferred_element_type=F32)


def _combine_kernel(off_ref, meta_hbm, ys_hbm, h_ref, wts_ref, gfin_ref, out_ref, meta_smem, y1_buf, y2_buf,
                    sem_meta, sem_rows, *, tm):
    i = pl.program_id(0)
    meta_cp = pltpu.make_async_copy(meta_hbm.at[i], meta_smem, sem_meta)
    meta_cp.start()
    meta_cp.wait()

    def row_copies(t):
        p1 = off_ref[meta_smem[0, t]] + meta_smem[2, t]
        p2 = off_ref[meta_smem[1, t]] + meta_smem[3, t]
        return (pltpu.make_async_copy(ys_hbm.at[pl.ds(p1, 1)], y1_buf.at[pl.ds(t, 1)], sem_rows),
                pltpu.make_async_copy(ys_hbm.at[pl.ds(p2, 1)], y2_buf.at[pl.ds(t, 1)], sem_rows))

    def issue(t, _):
        for cp in row_copies(t):
            cp.start()
        return 0

    def drain(t, _):
        for cp in row_copies(t):
            cp.wait()
        return 0

    lax.fori_loop(0, tm, issue, 0, unroll=8)
    lax.fori_loop(0, tm, drain, 0, unroll=8)
    w = wts_ref[...]
    hh = h_ref[...] + (w[:, 0:1] * y1_buf[...] + w[:, 1:2] * y2_buf[...])
    out_ref[...] = (hh * lax.rsqrt(jnp.mean(hh * hh, axis=-1, keepdims=True) + EPS)) * gfin_ref[...]


def _pad_heads(w, heads, dim):
    k = w.shape[0]
    w = w.reshape(k, heads, dim)
    return jnp.pad(w, ((0, 0), (0, 0), (0, HEAD_SLOT - dim))).reshape(k, heads * HEAD_SLOT)


def _swap_rope(w_rope):
    half = MLA_ROPE_DIM // 2
    return jnp.concatenate([-w_rope[..., half:], w_rope[..., :half]], axis=-1)


def _params(sem):
    return pltpu.CompilerParams(dimension_semantics=sem, vmem_limit_bytes=VMEM_LIMIT_BYTES)


def kernel(x, positions, g_mix, w_in, b_forget, g_cq, g_ckv, w_uq, w_uk, w_uv, w_o_mla, w_o_fox,
           w_out, g_ffn, w_group, b_group, w_router, b_router, w_e_gate, w_e_up, w_e_down, g_final):
    B, S, D = x.shape
    T = B * S
    assert D == D_MODEL
    tm = min(512, S)
    tk = ATT_TILE
    att_heads = 4
    att_tiles = 2 if S % (2 * ATT_TILE) == 0 else 1
    tm_e = 512
    assert S % tm == 0 and S % (att_tiles * ATT_TILE) == 0 and tm % tk == 0 and tk % CHUNK == 0

    cq_w, ckv_w, kr_w, fq_w, fk_w, fv_w, fl_w, ga_w, gb_w = jnp.split(
        w_in, [256, 384, 416, 928, 1440, 1952, 1960, 2984], axis=1)
    z64 = jnp.zeros((D, MLA_NOPE_DIM), F32)
    z32 = jnp.zeros((D, HEAD_SLOT - MLA_NOPE_DIM - MLA_ROPE_DIM), F32)
    fl_rep = jnp.pad(jnp.repeat(fl_w, AUG_SLOT, axis=1), ((0, 0), (0, LANES - FOX_HEADS * AUG_SLOT)))
    w1 = jnp.concatenate([cq_w, ckv_w,
                          jnp.concatenate([z64, kr_w, z32], axis=1),
                          jnp.concatenate([z64, _swap_rope(kr_w), z32], axis=1),
                          fq_w, fk_w, fv_w, fl_rep, ga_w, gb_w], axis=1).astype(BF16)
    assert w1.shape[1] == W1_COLS
    bf128 = jnp.pad(jnp.repeat(b_forget.astype(F32), AUG_SLOT), (0, LANES - FOX_HEADS * AUG_SLOT))[None, :]
    dq = MLA_NOPE_DIM + MLA_ROPE_DIM
    wq3 = w_uq.reshape(MLA_Q_RANK, MLA_HEADS, dq)
    wq_main = _pad_heads(w_uq, MLA_HEADS, dq).astype(BF16)
    wq_swap3 = jnp.concatenate([jnp.zeros((MLA_Q_RANK, MLA_HEADS, MLA_NOPE_DIM), F32),
                                _swap_rope(wq3[..., MLA_NOPE_DIM:])], axis=-1)
    wq_swap = _pad_heads(wq_swap3.reshape(MLA_Q_RANK, MLA_HEADS * dq), MLA_HEADS, dq).astype(BF16)
    wuk_pad = _pad_heads(w_uk, MLA_HEADS, MLA_NOPE_DIM).astype(BF16)
    half = MLA_ROPE_DIM // 2
    inv_freq = ROPE_THETA ** (-jnp.arange(half, dtype=F32) / half)
    freq128 = jnp.concatenate([jnp.zeros((MLA_NOPE_DIM,), F32), inv_freq, inv_freq,
                               jnp.zeros((HEAD_SLOT - dq,), F32)])[None, :]
    tri = (jnp.arange(tm)[:, None] >= jnp.arange(tm)[None, :]).astype(BF16)

    x2 = x.reshape(T, D)
    pos2 = positions.reshape(T, 1).astype(jnp.int32)
    row = lambda v: v.astype(F32)[None, :]

    full = lambda shape: pl.BlockSpec(shape, lambda i: (0,) * len(shape))
    tok = lambda w: pl.BlockSpec((tm, w), lambda i: (i, 0))
    n_tiles = T // tm
    nkv = S // tk

    q, k, vt, fq, fk, qaug, kaug, gates = pl.pallas_call(
        functools.partial(_proj_kernel, tiles_per_seq=S // tm, tk=tk,
                          scale_a=float(dq) ** -0.5, scale_b=float(FOX_HEAD_DIM) ** -0.5),
        grid=(n_tiles,),
        in_specs=[tok(D), tok(1), full((1, D)), full((D, W1_COLS)), full((1, MLA_Q_RANK)),
                  full((1, MLA_KV_RANK)), full((MLA_Q_RANK, MLA_HEADS * HEAD_SLOT)),
                  full((MLA_Q_RANK, MLA_HEADS * HEAD_SLOT)), full((MLA_KV_RANK, MLA_HEADS * HEAD_SLOT)),
                  full((MLA_KV_RANK, 512)), full((1, LANES)), full((1, LANES)), full((tm, tm))],
        out_specs=[tok(1024), tok(1024),
                   pl.BlockSpec((tm // tk, 1024, tk), lambda i: (i, 0, 0)),
                   tok(512), tok(512), tok(LANES), tok(LANES), tok(2 * D)],
        out_shape=[jax.ShapeDtypeStruct((T, 1024), BF16), jax.ShapeDtypeStruct((T, 1024), BF16),
                   jax.ShapeDtypeStruct((T // tk, 1024, tk), BF16),
                   jax.ShapeDtypeStruct((T, 512), BF16), jax.ShapeDtypeStruct((T, 512), BF16),
                   jax.ShapeDtypeStruct((T, LANES), BF16), jax.ShapeDtypeStruct((T, LANES), BF16),
                   jax.ShapeDtypeStruct((T, 2 * D), BF16)],
        scratch_shapes=[pltpu.VMEM((8, LANES), F32)],
        compiler_params=_params(("arbitrary",)),
        name="proj",
    )(x2, pos2, row(g_mix), w1, row(g_cq), row(g_ckv), wq_main, wq_swap, wuk_pad,
      w_uv.astype(BF16), bf128, freq128, tri)

    n_grp = MLA_HEADS // att_heads
    n_chains = att_heads * att_tiles
    seq_blk = lambda w: pl.BlockSpec((S, w), lambda b, g: (b, g))
    seq_all = lambda w: pl.BlockSpec((S, w), lambda b, g: (b, 0))
    att_scratch = [pltpu.VMEM((n_chains, 8, ATT_TILE), F32), pltpu.VMEM((n_chains, 8, ATT_TILE), F32),
                   pltpu.VMEM((n_chains, 64, ATT_TILE), F32)]
    o_a = pl.pallas_call(
        functools.partial(_attn_kernel, fox=False, heads=att_heads, tiles=att_tiles, seq=S),
        grid=(B, n_grp),
        in_specs=[seq_blk(att_heads * HEAD_SLOT), seq_blk(att_heads * HEAD_SLOT),
                  pl.BlockSpec((nkv, att_heads * 64, tk), lambda b, g: (b, g, 0))],
        out_specs=seq_blk(att_heads * 64),
        out_shape=jax.ShapeDtypeStruct((T, 512), BF16),
        scratch_shapes=att_scratch,
        compiler_params=_params(("arbitrary", "arbitrary")),
        name="attn_mla",
    )(q, k, vt)
    o_b = pl.pallas_call(
        functools.partial(_attn_kernel, fox=True, heads=att_heads, tiles=att_tiles, seq=S),
        grid=(B, n_grp),
        in_specs=[seq_blk(att_heads * 64), seq_blk(att_heads * 64), seq_all(LANES), seq_all(LANES),
                  pl.BlockSpec((nkv, att_heads * 64, tk), lambda b, g: (b, n_grp + g, 0))],
        out_specs=seq_blk(att_heads * 64),
        out_shape=jax.ShapeDtypeStruct((T, 512), BF16),
        scratch_shapes=att_scratch,
        compiler_params=_params(("arbitrary", "arbitrary")),
        name="attn_fox",
    )(fq, fk, qaug, kaug, vt)

    n_rt = 64
    wr = jnp.concatenate([w_router.T, w_group.T,
                          jnp.zeros((n_rt - N_EXPERTS - N_GROUPS, D), F32)], axis=0).astype(F32)
    wr_hi = wr.astype(BF16)
    wr_lo = (wr - wr_hi.astype(F32)).astype(BF16)
    br = jnp.concatenate([b_router.astype(F32), b_group.astype(F32),
                          jnp.full((n_rt - N_EXPERTS - N_GROUPS,), NEG_INF, F32)])[:, None]
    upper = (jnp.arange(tm)[:, None] < jnp.arange(tm)[None, :]).astype(BF16)
    h, hnp, meta, wts, counts = pl.pallas_call(
        _post_kernel,
        grid=(n_tiles,),
        in_specs=[tok(D), tok(512), tok(512), tok(2 * D), full((512, D)), full((512, D)),
                  full((D, D)), full((1, D)), full((n_rt, D)), full((n_rt, D)), full((n_rt, 1)),
                  full((tm, tm))],
        out_specs=[tok(D), tok(D // 2), pl.BlockSpec((1, 8, tm), lambda i: (i, 0, 0)), tok(LANES),
                   full((N_EXPERTS, LANES))],
        out_shape=[jax.ShapeDtypeStruct((T, D), F32), jax.ShapeDtypeStruct((T, D // 2), jnp.uint32),
                   jax.ShapeDtypeStruct((n_tiles, 8, tm), jnp.int32),
                   jax.ShapeDtypeStruct((T, LANES), F32),
                   jax.ShapeDtypeStruct((N_EXPERTS, LANES), F32)],
        scratch_shapes=[pltpu.VMEM((N_EXPERTS, LANES), F32)],
        compiler_params=_params(("arbitrary",)),
        name="post",
    )(x2, o_a, o_b, gates, w_o_mla.astype(BF16), w_o_fox.astype(BF16), w_out.astype(BF16),
      row(g_ffn), wr_hi, wr_lo, br, upper)

    n_tiles_e = (2 * T) // tm_e + N_EXPERTS
    cnt = counts[:, 0].astype(jnp.int32)
    tiles_per_e = (cnt + (tm_e - 1)) // tm_e
    cum_tiles = jnp.cumsum(tiles_per_e)
    off = ((cum_tiles - tiles_per_e) * tm_e).astype(jnp.int32)
    n_used = cum_tiles[-1:].astype(jnp.int32)
    tile_ids = jnp.minimum(jnp.arange(n_tiles_e, dtype=jnp.int32), n_used[0] - 1)
    tile_expert = jnp.sum(tile_ids[:, None] >= cum_tiles[None, :], axis=1).astype(jnp.int32)

    any_spec = pl.BlockSpec(memory_space=pl.ANY)
    dma_sem = pltpu.SemaphoreType.DMA(())
    xs = pl.pallas_call(
        functools.partial(_dispatch_kernel, tm=tm, tm_e=tm_e, n_tiles_e=n_tiles_e),
        grid_spec=pltpu.PrefetchScalarGridSpec(
            num_scalar_prefetch=3, grid=(n_tiles,),
            in_specs=[any_spec, any_spec], out_specs=any_spec,
            scratch_shapes=[pltpu.SMEM((8, tm), jnp.int32), pltpu.VMEM((tm_e, D // 2), jnp.uint32),
                            dma_sem, dma_sem, dma_sem]),
        out_shape=jax.ShapeDtypeStruct((n_tiles_e * tm_e, D // 2), jnp.uint32),
        compiler_params=_params(("arbitrary",)),
        name="dispatch",
    )(off, cnt, n_used, meta, hnp)

    used = lambda i, texp, nused: (jnp.minimum(i, nused[0] - 1), 0)
    ys = pl.pallas_call(
        _moe_kernel,
        grid_spec=pltpu.PrefetchScalarGridSpec(
            num_scalar_prefetch=2, grid=(n_tiles_e,),
            in_specs=[pl.BlockSpec((tm_e, D // 2), used),
                      pl.BlockSpec((1, D, EXPERT_FF), lambda i, texp, nused: (texp[i], 0, 0)),
                      pl.BlockSpec((1, D, EXPERT_FF), lambda i, texp, nused: (texp[i], 0, 0)),
                      pl.BlockSpec((1, EXPERT_FF, D), lambda i, texp, nused: (texp[i], 0, 0))],
            out_specs=pl.BlockSpec((tm_e, D), lambda i, texp, nused: (i, 0))),
        out_shape=jax.ShapeDtypeStruct((n_tiles_e * tm_e, D), F32),
        compiler_params=_params(("arbitrary",)),
        name="moe",
    )(tile_expert, n_used, xs, w_e_gate.astype(BF16), w_e_up.astype(BF16), w_e_down.astype(BF16))

    tok1 = lambda w: pl.BlockSpec((tm, w), lambda i, off_r: (i, 0))
    out = pl.pallas_call(
        functools.partial(_combine_kernel, tm=tm),
        grid_spec=pltpu.PrefetchScalarGridSpec(
            num_scalar_prefetch=1, grid=(n_tiles,),
            in_specs=[any_spec, any_spec, tok1(D), tok1(LANES),
                      pl.BlockSpec((1, D), lambda i, off_r: (0, 0))],
            out_specs=tok1(D),
            scratch_shapes=[pltpu.SMEM((8, tm), jnp.int32), pltpu.VMEM((tm, D), F32),
                            pltpu.VMEM((tm, D), F32), dma_sem, dma_sem]),
        out_shape=jax.ShapeDtypeStruct((T, D), F32),
        compiler_params=_params(("arbitrary",)),
        name="combine",
    )(off, meta, ys, h, wts, row(g_final))
    return out.reshape(B, S, D)
```

```python
import functools

import jax
import jax.numpy as jnp
from jax import lax
from jax.experimental import pallas as pl
from jax.experimental.pallas import tpu as pltpu

F32 = jnp.float32
BF16 = jnp.bfloat16

D_MODEL = 1024
CHUNK = 64
EPS = 1e-6
NEG_INF = -1e30
LOG2E = 1.4426950408889634

MLA_HEADS = 8
MLA_NOPE_DIM = 64
MLA_ROPE_DIM = 32
MLA_V_DIM = 64
MLA_Q_RANK = 256
MLA_KV_RANK = 128
ROPE_THETA = 10000.0
FOX_HEADS = 8
FOX_HEAD_DIM = 64
N_GROUPS = 4
EXPERTS_PER_GROUP = 8
N_EXPERTS = 32
EXPERT_FF = 256

LANES = 128
HEAD_SLOT = LANES
AUG_SLOT = 8
ATT_TILE = 256
ACC_ROWS = 64 + 16
ROW_DMA_UNROLL = 16
VMEM_LIMIT_BYTES = 56 * 1024 * 1024

_SEG_WIDTHS = dict(cq=MLA_Q_RANK, ckv=MLA_KV_RANK, krm=HEAD_SLOT, krs=HEAD_SLOT,
                   fq=512, fk=512, fv=512, flog=LANES, ga=D_MODEL, gb=D_MODEL)
_SEG = {}
_off = 0
for _name, _w in _SEG_WIDTHS.items():
    _SEG[_name] = (_off, _off + _w)
    _off += _w
W1_COLS = _off


def _split3(v):
    hi = v.astype(BF16).astype(F32)
    r = v - hi
    mid = r.astype(BF16).astype(F32)
    lo = (r - mid).astype(BF16).astype(F32)
    return hi, mid, lo


def _proj_kernel(x_ref, pos_ref, gmix_ref, w1_ref, gcq_ref, gckv_ref, wqm_ref, wqs_ref, wuk_ref,
                 wuv_ref, bf_ref, freq_ref, tri_ref,
                 q_ref, k_ref, vt_ref, fq_ref, fk_ref, qaug_ref, kaug_ref, gates_ref,
                 carry_ref, *, tiles_per_seq, tk, scale_a, scale_b):
    i = pl.program_id(0)
    tm = x_ref.shape[0]

    x = x_ref[...]
    ms = jnp.mean(x * x, axis=-1, keepdims=True)
    y = ((x * lax.rsqrt(ms + EPS)) * gmix_ref[...]).astype(BF16)

    def seg(name):
        a, b = _SEG[name]
        return jnp.dot(y, w1_ref[:, a:b], preferred_element_type=F32)

    ang = pos_ref[...].astype(F32) * freq_ref[...]
    cos = jnp.cos(ang)
    sin = jnp.sin(ang)

    cq = seg("cq")
    cqn = ((cq * lax.rsqrt(jnp.mean(cq * cq, axis=-1, keepdims=True) + EPS)) * gcq_ref[...]).astype(BF16)
    qm = jnp.dot(cqn, wqm_ref[...], preferred_element_type=F32)
    qs = jnp.dot(cqn, wqs_ref[...], preferred_element_type=F32)
    cos_q = cos * (scale_a * LOG2E)
    sin_q = sin * (scale_a * LOG2E)
    ckv = seg("ckv")
    ckvn = ((ckv * lax.rsqrt(jnp.mean(ckv * ckv, axis=-1, keepdims=True) + EPS)) * gckv_ref[...]).astype(BF16)
    knope = jnp.dot(ckvn, wuk_ref[...], preferred_element_type=F32)
    va = jnp.dot(ckvn, wuv_ref[...], preferred_element_type=F32)
    krope = seg("krm") * cos + seg("krs") * sin
    for h in range(MLA_HEADS):
        sl = slice(h * HEAD_SLOT, (h + 1) * HEAD_SLOT)
        q_ref[:, sl] = (qm[:, sl] * cos_q + qs[:, sl] * sin_q).astype(BF16)
        k_ref[:, sl] = (knope[:, sl] + krope).astype(BF16)

    fq_ref[...] = (seg("fq") * (scale_b * LOG2E)).astype(BF16)
    fk_ref[...] = seg("fk").astype(BF16)
    fv = seg("fv")
    va_t = va.T
    fv_t = fv.T
    for c in range(tm // tk):
        vt_ref[c, 0:512, :] = va_t[:, c * tk:(c + 1) * tk].astype(BF16)
        vt_ref[c, 512:1024, :] = fv_t[:, c * tk:(c + 1) * tk].astype(BF16)

    logf = jax.nn.log_sigmoid(seg("flog") + bf_ref[...])
    hi, mid, lo = _split3(logf)
    tri = tri_ref[...]
    cum = (jnp.dot(tri, hi.astype(BF16), preferred_element_type=F32)
           + jnp.dot(tri, mid.astype(BF16), preferred_element_type=F32)
           + jnp.dot(tri, lo.astype(BF16), preferred_element_type=F32))

    @pl.when(i % tiles_per_seq == 0)
    def _():
        carry_ref[...] = jnp.zeros_like(carry_ref)

    decay = cum + carry_ref[0:1, :]
    carry_ref[0:1, :] = decay[tm - 1:tm, :]
    d_hi, d_mid, d_lo = _split3(decay * LOG2E)
    lane = lax.broadcasted_iota(jnp.int32, (1, LANES), 1)
    l8 = lane % AUG_SLOT
    valid = lane < FOX_HEADS * AUG_SLOT
    one = jnp.ones_like(d_hi)
    zero = jnp.zeros_like(d_hi)
    qaug = jnp.where(l8 == 0, d_hi, jnp.where(l8 == 1, d_mid, jnp.where(l8 == 2, d_lo,
                     jnp.where(l8 < 6, one, zero))))
    kaug = jnp.where(l8 < 3, one, jnp.where(l8 == 3, -d_hi, jnp.where(l8 == 4, -d_mid,
                     jnp.where(l8 == 5, -d_lo, zero))))
    qaug_ref[...] = jnp.where(valid, qaug, zero).astype(BF16)
    kaug_ref[...] = jnp.where(valid, kaug, zero).astype(BF16)

    gates_ref[:, 0:D_MODEL] = jax.nn.sigmoid(seg("ga")).astype(BF16)
    gates_ref[:, D_MODEL:2 * D_MODEL] = jax.nn.sigmoid(seg("gb")).astype(BF16)


def _attn_kernel(*refs, fox, heads, tiles, seq):
    if fox:
        q_ref, k_ref, qaug_ref, kaug_ref, vt_ref, o_ref, m_scr, acc_scr, s_scr = refs
    else:
        q_ref, k_ref, vt_ref, o_ref, m_scr, acc_scr, s_scr = refs
    assert tiles % 2 == 0
    grp = pl.program_id(1)
    tq = tiles * ATT_TILE
    nq = seq // tq
    lane = lax.broadcasted_iota(jnp.int32, (1, LANES), 1)
    nt = (((1,), (1,)), ((), ()))
    chains = [(h, t) for h in range(heads) for t in range(tiles)]

    def q_block(i, _):
        q0 = pl.multiple_of(i * tq, tq)
        qops = {}
        for h in range(heads):
            for t in range(tiles):
                rows = pl.ds(pl.multiple_of(q0 + t * ATT_TILE, ATT_TILE), ATT_TILE)
                if fox:
                    blk = slice((h // 2) * LANES, (h // 2 + 1) * LANES)
                    qp = q_ref[rows, blk]
                    qa = qaug_ref[rows, :]
                    zq = jnp.zeros_like(qp)
                    qh = jnp.where(lane // FOX_HEAD_DIM == h % 2, qp, zq)
                    qah = jnp.where(lane // AUG_SLOT == heads * grp + h, qa, zq)
                    qops[(h, t)] = jnp.concatenate([qh, qah], axis=1)
                else:
                    qops[(h, t)] = q_ref[rows, h * HEAD_SLOT:(h + 1) * HEAD_SLOT]
        for c in range(len(chains)):
            m_scr[c] = jnp.full((8, ATT_TILE), NEG_INF, F32)
            acc_scr[c] = jnp.zeros((ACC_ROWS, ATT_TILE), F32)
        ones_rows = jnp.ones((ACC_ROWS - 64, ATT_TILE), BF16)

        def logits(j, slot, min_tile):
            krows = pl.ds(pl.multiple_of(j * ATT_TILE, ATT_TILE), ATT_TILE)
            kops = {}
            for h in range(heads):
                if fox:
                    if h % 2 == 0:
                        blk = slice((h // 2) * LANES, (h // 2 + 1) * LANES)
                        kops[h] = jnp.concatenate([k_ref[krows, blk], kaug_ref[krows, :]], axis=1)
                    else:
                        kops[h] = kops[h - 1]
                else:
                    kops[h] = k_ref[krows, h * HEAD_SLOT:(h + 1) * HEAD_SLOT]
            for c, ht in enumerate(chains):
                if ht[1] >= min_tile:
                    s_scr[slot, c] = lax.dot_general(kops[ht[0]], qops[ht], nt,
                                                     preferred_element_type=F32)

        def consume(j, slot, diag):
            active = [(c, ht) for c, ht in enumerate(chains) if diag is None or ht[1] >= diag]
            if diag is not None:
                kpos = lax.broadcasted_iota(jnp.int32, (ATT_TILE, 1), 0)
                qpos = lax.broadcasted_iota(jnp.int32, (1, ATT_TILE), 1)
                allowed = (kpos <= qpos) if fox else ((kpos // CHUNK) <= (qpos // CHUNK))
            p_all, alpha_all = {}, {}
            for c, ht in active:
                s = s_scr[slot, c]
                if diag is not None and ht[1] == diag:
                    s = jnp.where(allowed, s, NEG_INF)
                m = m_scr[c][0:1, :]
                m_new = jnp.maximum(m, jnp.max(s, axis=0, keepdims=True))
                alpha_all[c] = jnp.exp2(m - m_new)
                p_all[c] = jnp.exp2(s - m_new).astype(BF16)
                m_scr[c] = jnp.broadcast_to(m_new, (8, ATT_TILE))
            for c, ht in active:
                vt = jnp.concatenate([vt_ref[j, ht[0] * 64:(ht[0] + 1) * 64, :], ones_rows], axis=0)
                pv = jnp.dot(vt, p_all[c], preferred_element_type=F32)
                acc_scr[c] = alpha_all[c] * acc_scr[c] + pv

        n_full = i * tiles
        logits(0, 0, 0)

        def two_steps(u, _):
            j = 2 * u
            logits(j + 1, 1, 0)
            consume(j, 0, None)
            logits(j + 2, 0, 0)
            consume(j + 1, 1, None)
            return 0

        lax.fori_loop(0, n_full // 2, two_steps, 0)
        for d in range(tiles):
            if d + 1 < tiles:
                logits(n_full + d + 1, (d + 1) % 2, d + 1)
            consume(n_full + d, d % 2, d)
        for t in range(tiles):
            rows = pl.ds(pl.multiple_of(q0 + t * ATT_TILE, ATT_TILE), ATT_TILE)
            for hp in range(heads // 2):
                parts = []
                for h in (2 * hp, 2 * hp + 1):
                    acc = acc_scr[chains.index((h, t))]
                    parts.append(acc[0:64, :] / acc[64:65, :])
                o_t = jnp.concatenate(parts, axis=0)
                o_ref[rows, hp * LANES:(hp + 1) * LANES] = o_t.T.astype(BF16)
        return 0

    lax.fori_loop(0, nq, q_block, 0)


def _post_kernel(x_ref, oa_ref, ob_ref, gates_ref, woa_ref, wob_ref, wout_ref, gffn_ref,
                 wrh_ref, wrl_ref, br_ref, upper_ref, h_ref, hn_ref, meta_ref, wts_ref, cnt_ref, cnt_scr):
    tm = x_ref.shape[0]
    ya = jnp.dot(oa_ref[...], woa_ref[...], preferred_element_type=F32)
    yb = jnp.dot(ob_ref[...], wob_ref[...], preferred_element_type=F32)
    sa = gates_ref[:, 0:D_MODEL].astype(F32)
    sb = gates_ref[:, D_MODEL:2 * D_MODEL].astype(F32)
    mixed = (sa * ya + sb * yb).astype(BF16)
    h = x_ref[...] + jnp.dot(mixed, wout_ref[...], preferred_element_type=F32)
    h_ref[...] = h
    hn = (h * lax.rsqrt(jnp.mean(h * h, axis=-1, keepdims=True) + EPS)) * gffn_ref[...]
    hn_hi = hn.astype(BF16)
    hn_lo = (hn - hn_hi.astype(F32)).astype(BF16)
    hn_ref[...] = hn

    nt = (((1,), (1,)), ((), ()))
    wrh = wrh_ref[...]
    lt = (lax.dot_general(wrh, hn_hi, nt, preferred_element_type=F32)
          + lax.dot_general(wrh, hn_lo, nt, preferred_element_type=F32)
          + lax.dot_general(wrl_ref[...], hn_hi, nt, preferred_element_type=F32)) + br_ref[...]

    row8 = lax.broadcasted_iota(jnp.int32, (EXPERTS_PER_GROUP, tm), 0).astype(F32)
    big = jnp.float32(EXPERTS_PER_GROUP)
    lg = lt[N_EXPERTS:N_EXPERTS + 8, :]
    gmax = jnp.max(lg, axis=0, keepdims=True)
    g_p = 1.0 / jnp.sum(jnp.exp(lg - gmax), axis=0, keepdims=True)
    g_idx = jnp.min(jnp.where(lg == gmax, row8, big), axis=0, keepdims=True)
    sel = jnp.zeros((EXPERTS_PER_GROUP, tm), F32)
    for g in range(N_GROUPS):
        sel = jnp.where(g_idx == g, lt[g * EXPERTS_PER_GROUP:(g + 1) * EXPERTS_PER_GROUP, :], sel)
    m1 = jnp.max(sel, axis=0, keepdims=True)
    i1 = jnp.min(jnp.where(sel == m1, row8, big), axis=0, keepdims=True)
    sel2 = jnp.where(row8 == i1, -jnp.inf, sel)
    m2 = jnp.max(sel2, axis=0, keepdims=True)
    i2 = jnp.min(jnp.where(sel2 == m2, row8, big), axis=0, keepdims=True)
    r = jnp.exp(m2 - m1)
    w1 = g_p / (1.0 + r)
    w2 = g_p * r / (1.0 + r)
    e1 = g_idx * EXPERTS_PER_GROUP + i1
    e2 = g_idx * EXPERTS_PER_GROUP + i2
    @pl.when(pl.program_id(0) == 0)
    def _():
        cnt_scr[...] = jnp.zeros_like(cnt_scr)

    rowe = lax.broadcasted_iota(jnp.int32, (N_EXPERTS, tm), 0).astype(F32)
    is1 = rowe == e1
    is2 = rowe == e2
    onehot = jnp.where(is1 | is2, 1.0, 0.0)
    before = jnp.dot(onehot.astype(BF16), upper_ref[...], preferred_element_type=F32) + cnt_scr[:, 0:1]
    r1 = jnp.sum(jnp.where(is1, before, 0.0), axis=0, keepdims=True)
    r2 = jnp.sum(jnp.where(is2, before, 0.0), axis=0, keepdims=True)
    cnt_scr[...] = cnt_scr[...] + jnp.sum(onehot, axis=1, keepdims=True)
    cnt_ref[...] = cnt_scr[...]
    zrow = jnp.zeros_like(e1)
    meta_ref[0] = jnp.concatenate([e1, e2, r1, r2, zrow, zrow, zrow, zrow], axis=0).astype(jnp.int32)
    row = lax.broadcasted_iota(jnp.int32, (LANES, tm), 0)
    wts_t = jnp.where(row == 0, w1, jnp.where(row == 1, w2, 0.0))
    wts_ref[...] = wts_t.T


def _dispatch_kernel(off_ref, cnt_ref, nused_ref, pos_hbm, hn_ref, xs_hbm, pos_smem, zbuf, sem_meta,
                     sem_zero, sem_rows, *, tm, tm_e, n_tiles_e):
    i = pl.program_id(0)
    meta_cp = pltpu.make_async_copy(pos_hbm.at[pl.ds(pl.multiple_of(i * 2 * tm, 2 * tm), 2 * tm)],
                                    pos_smem, sem_meta)
    meta_cp.start()

    @pl.when(i == 0)
    def _():
        zbuf[...] = jnp.zeros_like(zbuf)
        shift = tm_e.bit_length() - 1

        def unused_tile(t, _):
            cp = pltpu.make_async_copy(zbuf, xs_hbm.at[pl.ds(pl.multiple_of(t * tm_e, tm_e), tm_e)], sem_zero)
            cp.start()
            cp.wait()
            return 0

        lax.fori_loop(nused_ref[0], n_tiles_e, unused_tile, 0)

        def tail_copy(e):
            n = cnt_ref[e]
            start = off_ref[e] + lax.shift_left(lax.shift_right_logical(jnp.maximum(n, 1) - 1, shift), shift)
            return n, pltpu.make_async_copy(zbuf, xs_hbm.at[pl.ds(pl.multiple_of(start, tm_e), tm_e)], sem_zero)

        for e in range(N_EXPERTS):
            n, cp = tail_copy(e)

            @pl.when(n > 0)
            def _():
                cp.start()
        for e in range(N_EXPERTS):
            n, cp = tail_copy(e)

            @pl.when(n > 0)
            def _():
                cp.wait()

    meta_cp.wait()

    def row_copies(t):
        src = hn_ref.at[pl.ds(t, 1)]
        return (pltpu.make_async_copy(src, xs_hbm.at[pl.ds(pos_smem[t], 1)], sem_rows),
                pltpu.make_async_copy(src, xs_hbm.at[pl.ds(pos_smem[tm + t], 1)], sem_rows))

    def issue(t, _):
        for cp in row_copies(t):
            cp.start()
        return 0

    def drain(t, _):
        for cp in row_copies(t):
            cp.wait()
        return 0

    lax.fori_loop(0, tm, issue, 0, unroll=ROW_DMA_UNROLL)
    lax.fori_loop(0, tm, drain, 0, unroll=ROW_DMA_UNROLL)


def _moe_kernel(texp_ref, nused_ref, xs_ref, wg_ref, wu_ref, wd_ref, ys_ref):
    del texp_ref

    @pl.when(pl.program_id(0) >= nused_ref[0])
    def _():
        ys_ref[...] = jnp.zeros_like(ys_ref)

    @pl.when(pl.program_id(0) < nused_ref[0])
    def _():
        x = xs_ref[...].astype(BF16)
        g = jnp.dot(x, wg_ref[0], preferred_element_type=F32)
        u = jnp.dot(x, wu_ref[0], preferred_element_type=F32)
        a = (jax.nn.silu(g) * u).astype(BF16)
        ys_ref[...] = jnp.dot(a, wd_ref[0], preferred_element_type=F32)


def _combine_kernel(pos_hbm, ys_hbm, h_ref, wts_ref, gfin_ref, out_ref, pos_smem, y1_buf, y2_buf,
                    sem_meta, sem_rows, *, tm):
    i = pl.program_id(0)
    meta_cp = pltpu.make_async_copy(pos_hbm.at[pl.ds(pl.multiple_of(i * 2 * tm, 2 * tm), 2 * tm)],
                                    pos_smem, sem_meta)
    meta_cp.start()
    meta_cp.wait()

    def row_copies(t):
        dst = pl.ds(t, 1)
        return (pltpu.make_async_copy(ys_hbm.at[pl.ds(pos_smem[t], 1)], y1_buf.at[dst], sem_rows),
                pltpu.make_async_copy(ys_hbm.at[pl.ds(pos_smem[tm + t], 1)], y2_buf.at[dst], sem_rows))

    def issue(t, _):
        for cp in row_copies(t):
            cp.start()
        return 0

    def drain(t, _):
        for cp in row_copies(t):
            cp.wait()
        return 0

    lax.fori_loop(0, tm, issue, 0, unroll=ROW_DMA_UNROLL)
    lax.fori_loop(0, tm, drain, 0, unroll=ROW_DMA_UNROLL)
    w = wts_ref[...]
    hh = h_ref[...] + (w[:, 0:1] * y1_buf[...] + w[:, 1:2] * y2_buf[...])
    out_ref[...] = (hh * lax.rsqrt(jnp.mean(hh * hh, axis=-1, keepdims=True) + EPS)) * gfin_ref[...]


def _pad_heads(w, heads, dim):
    k = w.shape[0]
    w = w.reshape(k, heads, dim)
    return jnp.pad(w, ((0, 0), (0, 0), (0, HEAD_SLOT - dim))).reshape(k, heads * HEAD_SLOT)


def _swap_rope(w_rope):
    half = MLA_ROPE_DIM // 2
    return jnp.concatenate([-w_rope[..., half:], w_rope[..., :half]], axis=-1)


def _params(sem):
    return pltpu.CompilerParams(dimension_semantics=sem, vmem_limit_bytes=VMEM_LIMIT_BYTES)


def kernel(x, positions, g_mix, w_in, b_forget, g_cq, g_ckv, w_uq, w_uk, w_uv, w_o_mla, w_o_fox,
           w_out, g_ffn, w_group, b_group, w_router, b_router, w_e_gate, w_e_up, w_e_down, g_final):
    B, S, D = x.shape
    T = B * S
    assert D == D_MODEL
    tm = min(512, S)
    tk = ATT_TILE
    att_heads = 4
    att_tiles = 2
    tm_e = 512
    assert S % tm == 0 and S % (att_tiles * ATT_TILE) == 0 and tm % tk == 0 and tk % CHUNK == 0

    cq_w, ckv_w, kr_w, fq_w, fk_w, fv_w, fl_w, ga_w, gb_w = jnp.split(
        w_in, [256, 384, 416, 928, 1440, 1952, 1960, 2984], axis=1)
    z64 = jnp.zeros((D, MLA_NOPE_DIM), F32)
    z32 = jnp.zeros((D, HEAD_SLOT - MLA_NOPE_DIM - MLA_ROPE_DIM), F32)
    fl_rep = jnp.pad(jnp.repeat(fl_w, AUG_SLOT, axis=1), ((0, 0), (0, LANES - FOX_HEADS * AUG_SLOT)))
    w1 = jnp.concatenate([cq_w, ckv_w,
                          jnp.concatenate([z64, kr_w, z32], axis=1),
                          jnp.concatenate([z64, _swap_rope(kr_w), z32], axis=1),
                          fq_w, fk_w, fv_w, fl_rep, ga_w, gb_w], axis=1).astype(BF16)
    assert w1.shape[1] == W1_COLS
    bf128 = jnp.pad(jnp.repeat(b_forget.astype(F32), AUG_SLOT), (0, LANES - FOX_HEADS * AUG_SLOT))[None, :]
    dq = MLA_NOPE_DIM + MLA_ROPE_DIM
    wq3 = w_uq.reshape(MLA_Q_RANK, MLA_HEADS, dq)
    wq_main = _pad_heads(w_uq, MLA_HEADS, dq).astype(BF16)
    wq_swap3 = jnp.concatenate([jnp.zeros((MLA_Q_RANK, MLA_HEADS, MLA_NOPE_DIM), F32),
                                _swap_rope(wq3[..., MLA_NOPE_DIM:])], axis=-1)
    wq_swap = _pad_heads(wq_swap3.reshape(MLA_Q_RANK, MLA_HEADS * dq), MLA_HEADS, dq).astype(BF16)
    wuk_pad = _pad_heads(w_uk, MLA_HEADS, MLA_NOPE_DIM).astype(BF16)
    half = MLA_ROPE_DIM // 2
    inv_freq = ROPE_THETA ** (-jnp.arange(half, dtype=F32) / half)
    freq128 = jnp.concatenate([jnp.zeros((MLA_NOPE_DIM,), F32), inv_freq, inv_freq,
                               jnp.zeros((HEAD_SLOT - dq,), F32)])[None, :]
    tri = (jnp.arange(tm)[:, None] >= jnp.arange(tm)[None, :]).astype(BF16)

    x2 = x.reshape(T, D)
    pos2 = positions.reshape(T, 1).astype(jnp.int32)
    row = lambda v: v.astype(F32)[None, :]

    full = lambda shape: pl.BlockSpec(shape, lambda i: (0,) * len(shape))
    tok = lambda w: pl.BlockSpec((tm, w), lambda i: (i, 0))
    n_tiles = T // tm
    nkv = S // tk

    q, k, vt, fq, fk, qaug, kaug, gates = pl.pallas_call(
        functools.partial(_proj_kernel, tiles_per_seq=S // tm, tk=tk,
                          scale_a=float(dq) ** -0.5, scale_b=float(FOX_HEAD_DIM) ** -0.5),
        grid=(n_tiles,),
        in_specs=[tok(D), tok(1), full((1, D)), full((D, W1_COLS)), full((1, MLA_Q_RANK)),
                  full((1, MLA_KV_RANK)), full((MLA_Q_RANK, MLA_HEADS * HEAD_SLOT)),
                  full((MLA_Q_RANK, MLA_HEADS * HEAD_SLOT)), full((MLA_KV_RANK, MLA_HEADS * HEAD_SLOT)),
                  full((MLA_KV_RANK, 512)), full((1, LANES)), full((1, LANES)), full((tm, tm))],
        out_specs=[tok(1024), tok(1024),
                   pl.BlockSpec((tm // tk, 1024, tk), lambda i: (i, 0, 0)),
                   tok(512), tok(512), tok(LANES), tok(LANES), tok(2 * D)],
        out_shape=[jax.ShapeDtypeStruct((T, 1024), BF16), jax.ShapeDtypeStruct((T, 1024), BF16),
                   jax.ShapeDtypeStruct((T // tk, 1024, tk), BF16),
                   jax.ShapeDtypeStruct((T, 512), BF16), jax.ShapeDtypeStruct((T, 512), BF16),
                   jax.ShapeDtypeStruct((T, LANES), BF16), jax.ShapeDtypeStruct((T, LANES), BF16),
                   jax.ShapeDtypeStruct((T, 2 * D), BF16)],
        scratch_shapes=[pltpu.VMEM((8, LANES), F32)],
        compiler_params=_params(("arbitrary",)),
        name="proj",
    )(x2, pos2, row(g_mix), w1, row(g_cq), row(g_ckv), wq_main, wq_swap, wuk_pad,
      w_uv.astype(BF16), bf128, freq128, tri)

    n_grp = MLA_HEADS // att_heads
    n_chains = att_heads * att_tiles
    seq_blk = lambda w: pl.BlockSpec((S, w), lambda b, g: (b, g))
    seq_all = lambda w: pl.BlockSpec((S, w), lambda b, g: (b, 0))
    att_scratch = [pltpu.VMEM((n_chains, 8, ATT_TILE), F32), pltpu.VMEM((n_chains, ACC_ROWS, ATT_TILE), F32),
                   pltpu.VMEM((2, n_chains, ATT_TILE, ATT_TILE), F32)]
    o_a = pl.pallas_call(
        functools.partial(_attn_kernel, fox=False, heads=att_heads, tiles=att_tiles, seq=S),
        grid=(B, n_grp),
        in_specs=[seq_blk(att_heads * HEAD_SLOT), seq_blk(att_heads * HEAD_SLOT),
                  pl.BlockSpec((nkv, att_heads * 64, tk), lambda b, g: (b, g, 0))],
        out_specs=seq_blk(att_heads * 64),
        out_shape=jax.ShapeDtypeStruct((T, 512), BF16),
        scratch_shapes=att_scratch,
        compiler_params=_params(("arbitrary", "arbitrary")),
        name="attn_mla",
    )(q, k, vt)
    o_b = pl.pallas_call(
        functools.partial(_attn_kernel, fox=True, heads=att_heads, tiles=att_tiles, seq=S),
        grid=(B, n_grp),
        in_specs=[seq_blk(att_heads * 64), seq_blk(att_heads * 64), seq_all(LANES), seq_all(LANES),
                  pl.BlockSpec((nkv, att_heads * 64, tk), lambda b, g: (b, n_grp + g, 0))],
        out_specs=seq_blk(att_heads * 64),
        out_shape=jax.ShapeDtypeStruct((T, 512), BF16),
        scratch_shapes=att_scratch,
        compiler_params=_params(("arbitrary", "arbitrary")),
        name="attn_fox",
    )(fq, fk, qaug, kaug, vt)

    n_rt = 64
    wr = jnp.concatenate([w_router.T, w_group.T,
                          jnp.zeros((n_rt - N_EXPERTS - N_GROUPS, D), F32)], axis=0).astype(F32)
    wr_hi = wr.astype(BF16)
    wr_lo = (wr - wr_hi.astype(F32)).astype(BF16)
    br = jnp.concatenate([b_router.astype(F32), b_group.astype(F32),
                          jnp.full((n_rt - N_EXPERTS - N_GROUPS,), NEG_INF, F32)])[:, None]
    upper = (jnp.arange(tm)[:, None] < jnp.arange(tm)[None, :]).astype(BF16)
    h, hn, meta, wts, counts = pl.pallas_call(
        _post_kernel,
        grid=(n_tiles,),
        in_specs=[tok(D), tok(512), tok(512), tok(2 * D), full((512, D)), full((512, D)),
                  full((D, D)), full((1, D)), full((n_rt, D)), full((n_rt, D)), full((n_rt, 1)),
                  full((tm, tm))],
        out_specs=[tok(D), tok(D),
                   pl.BlockSpec((1, 8, tm), lambda i: (i, 0, 0)), tok(LANES), full((N_EXPERTS, LANES))],
        out_shape=[jax.ShapeDtypeStruct((T, D), F32), jax.ShapeDtypeStruct((T, D), F32),
                   jax.ShapeDtypeStruct((n_tiles, 8, tm), jnp.int32),
                   jax.ShapeDtypeStruct((T, LANES), F32),
                   jax.ShapeDtypeStruct((N_EXPERTS, LANES), F32)],
        scratch_shapes=[pltpu.VMEM((N_EXPERTS, LANES), F32)],
        compiler_params=_params(("arbitrary",)),
        name="post",
    )(x2, o_a, o_b, gates, w_o_mla.astype(BF16), w_o_fox.astype(BF16), w_out.astype(BF16),
      row(g_ffn), wr_hi, wr_lo, br, upper)

    n_tiles_e = (2 * T) // tm_e + N_EXPERTS
    cnt = counts[:, 0].astype(jnp.int32)
    tiles_per_e = (cnt + (tm_e - 1)) // tm_e
    cum_tiles = jnp.cumsum(tiles_per_e)
    off = ((cum_tiles - tiles_per_e) * tm_e).astype(jnp.int32)
    n_used = cum_tiles[-1:].astype(jnp.int32)
    tile_ids = jnp.minimum(jnp.arange(n_tiles_e, dtype=jnp.int32), n_used[0] - 1)
    tile_expert = jnp.sum(tile_ids[:, None] >= cum_tiles[None, :], axis=1).astype(jnp.int32)

    experts = jnp.arange(N_EXPERTS, dtype=jnp.int32)
    run_start = lambda e: jnp.sum(jnp.where(e[..., None] == experts, off, 0), axis=-1)
    pos = jnp.stack([run_start(meta[:, 0, :]) + meta[:, 2, :],
                     run_start(meta[:, 1, :]) + meta[:, 3, :]], axis=1).reshape(-1)

    any_spec = pl.BlockSpec(memory_space=pl.ANY)
    dma_sem = pltpu.SemaphoreType.DMA(())
    xs = pl.pallas_call(
        functools.partial(_dispatch_kernel, tm=tm, tm_e=tm_e, n_tiles_e=n_tiles_e),
        grid_spec=pltpu.PrefetchScalarGridSpec(
            num_scalar_prefetch=3, grid=(n_tiles,),
            in_specs=[any_spec, pl.BlockSpec((tm, D), lambda i, *_: (i, 0))],
            out_specs=any_spec,
            scratch_shapes=[pltpu.SMEM((2 * tm,), jnp.int32), pltpu.VMEM((tm_e, D), F32),
                            dma_sem, dma_sem, dma_sem]),
        out_shape=jax.ShapeDtypeStruct((n_tiles_e * tm_e, D), F32),
        compiler_params=_params(("arbitrary",)),
        name="dispatch",
    )(off, cnt, n_used, pos, hn)

    used = lambda i, texp, nused: (jnp.minimum(i, nused[0] - 1), 0)
    ys = pl.pallas_call(
        _moe_kernel,
        grid_spec=pltpu.PrefetchScalarGridSpec(
            num_scalar_prefetch=2, grid=(n_tiles_e,),
            in_specs=[pl.BlockSpec((tm_e, D), used),
                      pl.BlockSpec((1, D, EXPERT_FF), lambda i, texp, nused: (texp[i], 0, 0)),
                      pl.BlockSpec((1, D, EXPERT_FF), lambda i, texp, nused: (texp[i], 0, 0)),
                      pl.BlockSpec((1, EXPERT_FF, D), lambda i, texp, nused: (texp[i], 0, 0))],
            out_specs=pl.BlockSpec((tm_e, D), lambda i, texp, nused: (i, 0))),
        out_shape=jax.ShapeDtypeStruct((n_tiles_e * tm_e, D), F32),
        compiler_params=_params(("arbitrary",)),
        name="moe",
    )(tile_expert, n_used, xs, w_e_gate.astype(BF16), w_e_up.astype(BF16), w_e_down.astype(BF16))

    out = pl.pallas_call(
        functools.partial(_combine_kernel, tm=tm),
        grid=(n_tiles,),
        in_specs=[any_spec, any_spec, tok(D), tok(LANES), full((1, D))],
        out_specs=tok(D),
        scratch_shapes=[pltpu.SMEM((2 * tm,), jnp.int32), pltpu.VMEM((tm, D), F32),
                        pltpu.VMEM((tm, D), F32), dma_sem, dma_sem],
        out_shape=jax.ShapeDtypeStruct((T, D), F32),
        compiler_params=_params(("arbitrary",)),
        name="combine",
    )(pos, ys, h, wts, row(g_final))
    return out.reshape(B, S, D)
```

```python
import functools

import jax
import jax.numpy as jnp
from jax import lax
from jax.experimental import pallas as pl
from jax.experimental.pallas import tpu as pltpu

F32 = jnp.float32
BF16 = jnp.bfloat16

D_MODEL = 1024
CHUNK = 64
EPS = 1e-6
NEG_INF = -1e30
LOG2E = 1.4426950408889634

MLA_HEADS = 8
MLA_NOPE_DIM = 64
MLA_ROPE_DIM = 32
MLA_V_DIM = 64
MLA_Q_RANK = 256
MLA_KV_RANK = 128
ROPE_THETA = 10000.0
FOX_HEADS = 8
FOX_HEAD_DIM = 64
N_GROUPS = 4
EXPERTS_PER_GROUP = 8
N_EXPERTS = 32
EXPERT_FF = 256

LANES = 128
HEAD_SLOT = LANES
AUG_SLOT = 8
ATT_TILE = 256
ACC_ROWS = 64 + 16
ROW_CHUNKS = D_MODEL // LANES
ROW_DMA_UNROLL = 16
VMEM_LIMIT_BYTES = 56 * 1024 * 1024

_SEG_WIDTHS = dict(cq=MLA_Q_RANK, ckv=MLA_KV_RANK, krm=HEAD_SLOT, krs=HEAD_SLOT,
                   fq=512, fk=512, fv=512, flog=LANES, ga=D_MODEL, gb=D_MODEL)
_SEG = {}
_off = 0
for _name, _w in _SEG_WIDTHS.items():
    _SEG[_name] = (_off, _off + _w)
    _off += _w
W1_COLS = _off


def _split3(v):
    hi = v.astype(BF16).astype(F32)
    r = v - hi
    mid = r.astype(BF16).astype(F32)
    lo = (r - mid).astype(BF16).astype(F32)
    return hi, mid, lo


def _proj_kernel(x_ref, pos_ref, gmix_ref, w1_ref, gcq_ref, gckv_ref, wqm_ref, wqs_ref, wuk_ref,
                 wuv_ref, bf_ref, freq_ref, tri_ref,
                 q_ref, k_ref, vt_ref, fq_ref, fk_ref, qaug_ref, kaug_ref, gates_ref,
                 carry_ref, *, tiles_per_seq, tk, scale_a, scale_b):
    i = pl.program_id(0)
    tm = x_ref.shape[0]

    x = x_ref[...]
    ms = jnp.mean(x * x, axis=-1, keepdims=True)
    y = ((x * lax.rsqrt(ms + EPS)) * gmix_ref[...]).astype(BF16)

    def seg(name):
        a, b = _SEG[name]
        return jnp.dot(y, w1_ref[:, a:b], preferred_element_type=F32)

    ang = pos_ref[...].astype(F32) * freq_ref[...]
    cos = jnp.cos(ang)
    sin = jnp.sin(ang)

    cq = seg("cq")
    cqn = ((cq * lax.rsqrt(jnp.mean(cq * cq, axis=-1, keepdims=True) + EPS)) * gcq_ref[...]).astype(BF16)
    qm = jnp.dot(cqn, wqm_ref[...], preferred_element_type=F32)
    qs = jnp.dot(cqn, wqs_ref[...], preferred_element_type=F32)
    cos_q = cos * (scale_a * LOG2E)
    sin_q = sin * (scale_a * LOG2E)
    ckv = seg("ckv")
    ckvn = ((ckv * lax.rsqrt(jnp.mean(ckv * ckv, axis=-1, keepdims=True) + EPS)) * gckv_ref[...]).astype(BF16)
    knope = jnp.dot(ckvn, wuk_ref[...], preferred_element_type=F32)
    va = jnp.dot(ckvn, wuv_ref[...], preferred_element_type=F32)
    krope = seg("krm") * cos + seg("krs") * sin
    for h in range(MLA_HEADS):
        sl = slice(h * HEAD_SLOT, (h + 1) * HEAD_SLOT)
        q_ref[:, sl] = (qm[:, sl] * cos_q + qs[:, sl] * sin_q).astype(BF16)
        k_ref[:, sl] = (knope[:, sl] + krope).astype(BF16)

    fq_ref[...] = (seg("fq") * (scale_b * LOG2E)).astype(BF16)
    fk_ref[...] = seg("fk").astype(BF16)
    fv = seg("fv")
    va_t = va.T
    fv_t = fv.T
    for c in range(tm // tk):
        vt_ref[c, 0:512, :] = va_t[:, c * tk:(c + 1) * tk].astype(BF16)
        vt_ref[c, 512:1024, :] = fv_t[:, c * tk:(c + 1) * tk].astype(BF16)

    logf = jax.nn.log_sigmoid(seg("flog") + bf_ref[...])
    hi, mid, lo = _split3(logf)
    tri = tri_ref[...]
    cum = (jnp.dot(tri, hi.astype(BF16), preferred_element_type=F32)
           + jnp.dot(tri, mid.astype(BF16), preferred_element_type=F32)
           + jnp.dot(tri, lo.astype(BF16), preferred_element_type=F32))

    @pl.when(i % tiles_per_seq == 0)
    def _():
        carry_ref[...] = jnp.zeros_like(carry_ref)

    decay = cum + carry_ref[0:1, :]
    carry_ref[0:1, :] = decay[tm - 1:tm, :]
    d_hi, d_mid, d_lo = _split3(decay * LOG2E)
    lane = lax.broadcasted_iota(jnp.int32, (1, LANES), 1)
    l8 = lane % AUG_SLOT
    valid = lane < FOX_HEADS * AUG_SLOT
    one = jnp.ones_like(d_hi)
    zero = jnp.zeros_like(d_hi)
    qaug = jnp.where(l8 == 0, d_hi, jnp.where(l8 == 1, d_mid, jnp.where(l8 == 2, d_lo,
                     jnp.where(l8 < 6, one, zero))))
    kaug = jnp.where(l8 < 3, one, jnp.where(l8 == 3, -d_hi, jnp.where(l8 == 4, -d_mid,
                     jnp.where(l8 == 5, -d_lo, zero))))
    qaug_ref[...] = jnp.where(valid, qaug, zero).astype(BF16)
    kaug_ref[...] = jnp.where(valid, kaug, zero).astype(BF16)

    gates_ref[:, 0:D_MODEL] = jax.nn.sigmoid(seg("ga")).astype(BF16)
    gates_ref[:, D_MODEL:2 * D_MODEL] = jax.nn.sigmoid(seg("gb")).astype(BF16)


def _attn_kernel(*refs, fox, heads, tiles, seq):
    if fox:
        q_ref, k_ref, qaug_ref, kaug_ref, vt_ref, o_ref, m_scr, acc_scr, s_scr = refs
    else:
        q_ref, k_ref, vt_ref, o_ref, m_scr, acc_scr, s_scr = refs
    assert tiles % 2 == 0
    grp = pl.program_id(1)
    tq = tiles * ATT_TILE
    nq = seq // tq
    lane = lax.broadcasted_iota(jnp.int32, (1, LANES), 1)
    nt = (((1,), (1,)), ((), ()))
    chains = [(h, t) for h in range(heads) for t in range(tiles)]

    def q_block(i, _):
        q0 = pl.multiple_of(i * tq, tq)
        qops = {}
        for h in range(heads):
            for t in range(tiles):
                rows = pl.ds(pl.multiple_of(q0 + t * ATT_TILE, ATT_TILE), ATT_TILE)
                if fox:
                    blk = slice((h // 2) * LANES, (h // 2 + 1) * LANES)
                    qp = q_ref[rows, blk]
                    qa = qaug_ref[rows, :]
                    zq = jnp.zeros_like(qp)
                    qh = jnp.where(lane // FOX_HEAD_DIM == h % 2, qp, zq)
                    qah = jnp.where(lane // AUG_SLOT == heads * grp + h, qa, zq)
                    qops[(h, t)] = jnp.concatenate([qh, qah], axis=1)
                else:
                    qops[(h, t)] = q_ref[rows, h * HEAD_SLOT:(h + 1) * HEAD_SLOT]
        for c in range(len(chains)):
            m_scr[c] = jnp.full((8, ATT_TILE), NEG_INF, F32)
            acc_scr[c] = jnp.zeros((ACC_ROWS, ATT_TILE), F32)
        ones_rows = jnp.ones((ACC_ROWS - 64, ATT_TILE), BF16)

        def logits(j, slot, min_tile):
            krows = pl.ds(pl.multiple_of(j * ATT_TILE, ATT_TILE), ATT_TILE)
            kops = {}
            for h in range(heads):
                if fox:
                    if h % 2 == 0:
                        blk = slice((h // 2) * LANES, (h // 2 + 1) * LANES)
                        kops[h] = jnp.concatenate([k_ref[krows, blk], kaug_ref[krows, :]], axis=1)
                    else:
                        kops[h] = kops[h - 1]
                else:
                    kops[h] = k_ref[krows, h * HEAD_SLOT:(h + 1) * HEAD_SLOT]
            for c, ht in enumerate(chains):
                if ht[1] >= min_tile:
                    s_scr[slot, c] = lax.dot_general(kops[ht[0]], qops[ht], nt,
                                                     preferred_element_type=F32)

        def consume(j, slot, diag):
            active = [(c, ht) for c, ht in enumerate(chains) if diag is None or ht[1] >= diag]
            if diag is not None:
                kpos = lax.broadcasted_iota(jnp.int32, (ATT_TILE, 1), 0)
                qpos = lax.broadcasted_iota(jnp.int32, (1, ATT_TILE), 1)
                allowed = (kpos <= qpos) if fox else ((kpos // CHUNK) <= (qpos // CHUNK))
            p_all, alpha_all = {}, {}
            for c, ht in active:
                s = s_scr[slot, c]
                if diag is not None and ht[1] == diag:
                    s = jnp.where(allowed, s, NEG_INF)
                m = m_scr[c][0:1, :]
                m_new = jnp.maximum(m, jnp.max(s, axis=0, keepdims=True))
                alpha_all[c] = jnp.exp2(m - m_new)
                p_all[c] = jnp.exp2(s - m_new).astype(BF16)
                m_scr[c] = jnp.broadcast_to(m_new, (8, ATT_TILE))
            for c, ht in active:
                vt = jnp.concatenate([vt_ref[j, ht[0] * 64:(ht[0] + 1) * 64, :], ones_rows], axis=0)
                pv = jnp.dot(vt, p_all[c], preferred_element_type=F32)
                acc_scr[c] = alpha_all[c] * acc_scr[c] + pv

        n_full = i * tiles
        logits(0, 0, 0)

        def two_steps(u, _):
            j = 2 * u
            logits(j + 1, 1, 0)
            consume(j, 0, None)
            logits(j + 2, 0, 0)
            consume(j + 1, 1, None)
            return 0

        lax.fori_loop(0, n_full // 2, two_steps, 0)
        for d in range(tiles):
            if d + 1 < tiles:
                logits(n_full + d + 1, (d + 1) % 2, d + 1)
            consume(n_full + d, d % 2, d)
        for t in range(tiles):
            rows = pl.ds(pl.multiple_of(q0 + t * ATT_TILE, ATT_TILE), ATT_TILE)
            for hp in range(heads // 2):
                parts = []
                for h in (2 * hp, 2 * hp + 1):
                    acc = acc_scr[chains.index((h, t))]
                    parts.append(acc[0:64, :] / acc[64:65, :])
                o_t = jnp.concatenate(parts, axis=0)
                o_ref[rows, hp * LANES:(hp + 1) * LANES] = o_t.T.astype(BF16)
        return 0

    lax.fori_loop(0, nq, q_block, 0)


def _post_kernel(x_ref, oa_ref, ob_ref, gates_ref, woa_ref, wob_ref, wout_ref, gffn_ref,
                 wrh_ref, wrl_ref, br_ref, upper_ref, h_ref, hn_ref, meta_ref, wts_ref, cnt_ref, cnt_scr):
    tm = x_ref.shape[0]
    ya = jnp.dot(oa_ref[...], woa_ref[...], preferred_element_type=F32)
    yb = jnp.dot(ob_ref[...], wob_ref[...], preferred_element_type=F32)
    sa = gates_ref[:, 0:D_MODEL].astype(F32)
    sb = gates_ref[:, D_MODEL:2 * D_MODEL].astype(F32)
    mixed = (sa * ya + sb * yb).astype(BF16)
    h = x_ref[...] + jnp.dot(mixed, wout_ref[...], preferred_element_type=F32)
    h_ref[...] = h
    hn = (h * lax.rsqrt(jnp.mean(h * h, axis=-1, keepdims=True) + EPS)) * gffn_ref[...]
    hn_hi = hn.astype(BF16)
    hn_lo = (hn - hn_hi.astype(F32)).astype(BF16)
    for s in range(ROW_CHUNKS):
        hn_ref[pl.ds(s, tm, stride=ROW_CHUNKS), :] = hn[:, s * LANES:(s + 1) * LANES]

    nt = (((1,), (1,)), ((), ()))
    wrh = wrh_ref[...]
    lt = (lax.dot_general(wrh, hn_hi, nt, preferred_element_type=F32)
          + lax.dot_general(wrh, hn_lo, nt, preferred_element_type=F32)
          + lax.dot_general(wrl_ref[...], hn_hi, nt, preferred_element_type=F32)) + br_ref[...]

    row8 = lax.broadcasted_iota(jnp.int32, (EXPERTS_PER_GROUP, tm), 0).astype(F32)
    big = jnp.float32(EXPERTS_PER_GROUP)
    lg = lt[N_EXPERTS:N_EXPERTS + 8, :]
    gmax = jnp.max(lg, axis=0, keepdims=True)
    g_p = 1.0 / jnp.sum(jnp.exp(lg - gmax), axis=0, keepdims=True)
    g_idx = jnp.min(jnp.where(lg == gmax, row8, big), axis=0, keepdims=True)
    sel = jnp.zeros((EXPERTS_PER_GROUP, tm), F32)
    for g in range(N_GROUPS):
        sel = jnp.where(g_idx == g, lt[g * EXPERTS_PER_GROUP:(g + 1) * EXPERTS_PER_GROUP, :], sel)
    m1 = jnp.max(sel, axis=0, keepdims=True)
    i1 = jnp.min(jnp.where(sel == m1, row8, big), axis=0, keepdims=True)
    sel2 = jnp.where(row8 == i1, -jnp.inf, sel)
    m2 = jnp.max(sel2, axis=0, keepdims=True)
    i2 = jnp.min(jnp.where(sel2 == m2, row8, big), axis=0, keepdims=True)
    r = jnp.exp(m2 - m1)
    w1 = g_p / (1.0 + r)
    w2 = g_p * r / (1.0 + r)
    e1 = g_idx * EXPERTS_PER_GROUP + i1
    e2 = g_idx * EXPERTS_PER_GROUP + i2
    @pl.when(pl.program_id(0) == 0)
    def _():
        cnt_scr[...] = jnp.zeros_like(cnt_scr)

    rowe = lax.broadcasted_iota(jnp.int32, (N_EXPERTS, tm), 0).astype(F32)
    is1 = rowe == e1
    is2 = rowe == e2
    onehot = jnp.where(is1 | is2, 1.0, 0.0)
    before = jnp.dot(onehot.astype(BF16), upper_ref[...], preferred_element_type=F32) + cnt_scr[:, 0:1]
    r1 = jnp.sum(jnp.where(is1, before, 0.0), axis=0, keepdims=True)
    r2 = jnp.sum(jnp.where(is2, before, 0.0), axis=0, keepdims=True)
    cnt_scr[...] = cnt_scr[...] + jnp.sum(onehot, axis=1, keepdims=True)
    cnt_ref[...] = cnt_scr[...]
    zrow = jnp.zeros_like(e1)
    meta_ref[0] = jnp.concatenate([e1, e2, r1, r2, zrow, zrow, zrow, zrow], axis=0).astype(jnp.int32)
    row = lax.broadcasted_iota(jnp.int32, (LANES, tm), 0)
    wts_t = jnp.where(row == 0, w1, jnp.where(row == 1, w2, 0.0))
    wts_ref[...] = wts_t.T


def _dispatch_kernel(off_ref, cnt_ref, nused_ref, pos_hbm, hn_ref, xs_hbm, pos_smem, zbuf, sem_meta,
                     sem_zero, sem_rows, *, tm, tm_e, n_tiles_e):
    i = pl.program_id(0)
    meta_cp = pltpu.make_async_copy(pos_hbm.at[pl.ds(pl.multiple_of(i * 2 * tm, 2 * tm), 2 * tm)],
                                    pos_smem, sem_meta)
    meta_cp.start()

    @pl.when(i == 0)
    def _():
        zbuf[...] = jnp.zeros_like(zbuf)
        shift = tm_e.bit_length() - 1

        def unused_tile(t, _):
            cp = pltpu.make_async_copy(zbuf, xs_hbm.at[pl.ds(pl.multiple_of(t * tm_e, tm_e), tm_e)], sem_zero)
            cp.start()
            cp.wait()
            return 0

        lax.fori_loop(nused_ref[0], n_tiles_e, unused_tile, 0)

        def tail_copy(e):
            n = cnt_ref[e]
            start = off_ref[e] + lax.shift_left(lax.shift_right_logical(jnp.maximum(n, 1) - 1, shift), shift)
            return n, pltpu.make_async_copy(zbuf, xs_hbm.at[pl.ds(pl.multiple_of(start, tm_e), tm_e)], sem_zero)

        for e in range(N_EXPERTS):
            n, cp = tail_copy(e)

            @pl.when(n > 0)
            def _():
                cp.start()
        for e in range(N_EXPERTS):
            n, cp = tail_copy(e)

            @pl.when(n > 0)
            def _():
                cp.wait()

    meta_cp.wait()

    def row_copies(t):
        src = hn_ref.at[t]
        return (pltpu.make_async_copy(src, xs_hbm.at[pos_smem[t]], sem_rows),
                pltpu.make_async_copy(src, xs_hbm.at[pos_smem[tm + t]], sem_rows))

    def issue(t, _):
        for cp in row_copies(t):
            cp.start()
        return 0

    def drain(t, _):
        for cp in row_copies(t):
            cp.wait()
        return 0

    lax.fori_loop(0, tm, issue, 0, unroll=ROW_DMA_UNROLL)
    lax.fori_loop(0, tm, drain, 0, unroll=ROW_DMA_UNROLL)


def _moe_kernel(texp_ref, nused_ref, xs_ref, wg_ref, wu_ref, wd_ref, ys_ref):
    del texp_ref

    @pl.when(pl.program_id(0) >= nused_ref[0])
    def _():
        ys_ref[...] = jnp.zeros_like(ys_ref)

    @pl.when(pl.program_id(0) < nused_ref[0])
    def _():
        tm_e = xs_ref.shape[0] // ROW_CHUNKS
        x = jnp.concatenate([xs_ref[pl.ds(s, tm_e, stride=ROW_CHUNKS), :] for s in range(ROW_CHUNKS)],
                            axis=1).astype(BF16)
        g = jnp.dot(x, wg_ref[0].astype(BF16), preferred_element_type=F32)
        u = jnp.dot(x, wu_ref[0].astype(BF16), preferred_element_type=F32)
        a = (jax.nn.silu(g) * u).astype(BF16)
        y = jnp.dot(a, wd_ref[0].astype(BF16), preferred_element_type=F32)
        for s in range(ROW_CHUNKS):
            ys_ref[pl.ds(s, tm_e, stride=ROW_CHUNKS), :] = y[:, s * LANES:(s + 1) * LANES]


def _combine_kernel(pos_hbm, ys_hbm, h_ref, wts_ref, gfin_ref, out_ref, pos_smem, y1_buf, y2_buf,
                    sem_meta, sem_rows, *, tm):
    i = pl.program_id(0)
    meta_cp = pltpu.make_async_copy(pos_hbm.at[pl.ds(pl.multiple_of(i * 2 * tm, 2 * tm), 2 * tm)],
                                    pos_smem, sem_meta)
    meta_cp.start()
    meta_cp.wait()

    def row_copies(t):
        dst = pl.ds(pl.multiple_of(t * ROW_CHUNKS, ROW_CHUNKS), ROW_CHUNKS)
        return (pltpu.make_async_copy(ys_hbm.at[pos_smem[t]], y1_buf.at[dst], sem_rows),
                pltpu.make_async_copy(ys_hbm.at[pos_smem[tm + t]], y2_buf.at[dst], sem_rows))

    def issue(t, _):
        for cp in row_copies(t):
            cp.start()
        return 0

    def drain(t, _):
        for cp in row_copies(t):
            cp.wait()
        return 0

    lax.fori_loop(0, tm, issue, 0, unroll=ROW_DMA_UNROLL)
    lax.fori_loop(0, tm, drain, 0, unroll=ROW_DMA_UNROLL)
    w = wts_ref[...]
    w1 = w[:, 0:1]
    w2 = w[:, 1:2]
    y = jnp.concatenate([w1 * y1_buf[pl.ds(s, tm, stride=ROW_CHUNKS), :]
                         + w2 * y2_buf[pl.ds(s, tm, stride=ROW_CHUNKS), :] for s in range(ROW_CHUNKS)], axis=1)
    hh = h_ref[...] + y
    out_ref[...] = (hh * lax.rsqrt(jnp.mean(hh * hh, axis=-1, keepdims=True) + EPS)) * gfin_ref[...]


def _pad_heads(w, heads, dim):
    k = w.shape[0]
    w = w.reshape(k, heads, dim)
    return jnp.pad(w, ((0, 0), (0, 0), (0, HEAD_SLOT - dim))).reshape(k, heads * HEAD_SLOT)


def _swap_rope(w_rope):
    half = MLA_ROPE_DIM // 2
    return jnp.concatenate([-w_rope[..., half:], w_rope[..., :half]], axis=-1)


def _params(sem):
    return pltpu.CompilerParams(dimension_semantics=sem, vmem_limit_bytes=VMEM_LIMIT_BYTES)


def kernel(x, positions, g_mix, w_in, b_forget, g_cq, g_ckv, w_uq, w_uk, w_uv, w_o_mla, w_o_fox,
           w_out, g_ffn, w_group, b_group, w_router, b_router, w_e_gate, w_e_up, w_e_down, g_final):
    B, S, D = x.shape
    T = B * S
    assert D == D_MODEL
    tm = min(512, S)
    tk = ATT_TILE
    att_heads = 4
    att_tiles = 4
    tm_e = 512
    assert S % tm == 0 and S % (att_tiles * ATT_TILE) == 0 and tm % tk == 0 and tk % CHUNK == 0

    cq_w, ckv_w, kr_w, fq_w, fk_w, fv_w, fl_w, ga_w, gb_w = jnp.split(
        w_in, [256, 384, 416, 928, 1440, 1952, 1960, 2984], axis=1)
    z64 = jnp.zeros((D, MLA_NOPE_DIM), F32)
    z32 = jnp.zeros((D, HEAD_SLOT - MLA_NOPE_DIM - MLA_ROPE_DIM), F32)
    fl_rep = jnp.pad(jnp.repeat(fl_w, AUG_SLOT, axis=1), ((0, 0), (0, LANES - FOX_HEADS * AUG_SLOT)))
    w1 = jnp.concatenate([cq_w, ckv_w,
                          jnp.concatenate([z64, kr_w, z32], axis=1),
                          jnp.concatenate([z64, _swap_rope(kr_w), z32], axis=1),
                          fq_w, fk_w, fv_w, fl_rep, ga_w, gb_w], axis=1).astype(BF16)
    assert w1.shape[1] == W1_COLS
    bf128 = jnp.pad(jnp.repeat(b_forget.astype(F32), AUG_SLOT), (0, LANES - FOX_HEADS * AUG_SLOT))[None, :]
    dq = MLA_NOPE_DIM + MLA_ROPE_DIM
    wq3 = w_uq.reshape(MLA_Q_RANK, MLA_HEADS, dq)
    wq_main = _pad_heads(w_uq, MLA_HEADS, dq).astype(BF16)
    wq_swap3 = jnp.concatenate([jnp.zeros((MLA_Q_RANK, MLA_HEADS, MLA_NOPE_DIM), F32),
                                _swap_rope(wq3[..., MLA_NOPE_DIM:])], axis=-1)
    wq_swap = _pad_heads(wq_swap3.reshape(MLA_Q_RANK, MLA_HEADS * dq), MLA_HEADS, dq).astype(BF16)
    wuk_pad = _pad_heads(w_uk, MLA_HEADS, MLA_NOPE_DIM).astype(BF16)
    half = MLA_ROPE_DIM // 2
    inv_freq = ROPE_THETA ** (-jnp.arange(half, dtype=F32) / half)
    freq128 = jnp.concatenate([jnp.zeros((MLA_NOPE_DIM,), F32), inv_freq, inv_freq,
                               jnp.zeros((HEAD_SLOT - dq,), F32)])[None, :]
    tri = (jnp.arange(tm)[:, None] >= jnp.arange(tm)[None, :]).astype(BF16)

    x2 = x.reshape(T, D)
    pos2 = positions.reshape(T, 1).astype(jnp.int32)
    row = lambda v: v.astype(F32)[None, :]

    full = lambda shape: pl.BlockSpec(shape, lambda i: (0,) * len(shape))
    tok = lambda w: pl.BlockSpec((tm, w), lambda i: (i, 0))
    n_tiles = T // tm
    nkv = S // tk

    q, k, vt, fq, fk, qaug, kaug, gates = pl.pallas_call(
        functools.partial(_proj_kernel, tiles_per_seq=S // tm, tk=tk,
                          scale_a=float(dq) ** -0.5, scale_b=float(FOX_HEAD_DIM) ** -0.5),
        grid=(n_tiles,),
        in_specs=[tok(D), tok(1), full((1, D)), full((D, W1_COLS)), full((1, MLA_Q_RANK)),
                  full((1, MLA_KV_RANK)), full((MLA_Q_RANK, MLA_HEADS * HEAD_SLOT)),
                  full((MLA_Q_RANK, MLA_HEADS * HEAD_SLOT)), full((MLA_KV_RANK, MLA_HEADS * HEAD_SLOT)),
                  full((MLA_KV_RANK, 512)), full((1, LANES)), full((1, LANES)), full((tm, tm))],
        out_specs=[tok(1024), tok(1024),
                   pl.BlockSpec((tm // tk, 1024, tk), lambda i: (i, 0, 0)),
                   tok(512), tok(512), tok(LANES), tok(LANES), tok(2 * D)],
        out_shape=[jax.ShapeDtypeStruct((T, 1024), BF16), jax.ShapeDtypeStruct((T, 1024), BF16),
                   jax.ShapeDtypeStruct((T // tk, 1024, tk), BF16),
                   jax.ShapeDtypeStruct((T, 512), BF16), jax.ShapeDtypeStruct((T, 512), BF16),
                   jax.ShapeDtypeStruct((T, LANES), BF16), jax.ShapeDtypeStruct((T, LANES), BF16),
                   jax.ShapeDtypeStruct((T, 2 * D), BF16)],
        scratch_shapes=[pltpu.VMEM((8, LANES), F32)],
        compiler_params=_params(("arbitrary",)),
        name="proj",
    )(x2, pos2, row(g_mix), w1, row(g_cq), row(g_ckv), wq_main, wq_swap, wuk_pad,
      w_uv.astype(BF16), bf128, freq128, tri)

    n_grp = MLA_HEADS // att_heads
    n_chains = att_heads * att_tiles
    seq_blk = lambda w: pl.BlockSpec((S, w), lambda b, g: (b, g))
    seq_all = lambda w: pl.BlockSpec((S, w), lambda b, g: (b, 0))
    att_scratch = [pltpu.VMEM((n_chains, 8, ATT_TILE), F32), pltpu.VMEM((n_chains, ACC_ROWS, ATT_TILE), F32),
                   pltpu.VMEM((2, n_chains, ATT_TILE, ATT_TILE), F32)]
    o_a = pl.pallas_call(
        functools.partial(_attn_kernel, fox=False, heads=att_heads, tiles=att_tiles, seq=S),
        grid=(B, n_grp),
        in_specs=[seq_blk(att_heads * HEAD_SLOT), seq_blk(att_heads * HEAD_SLOT),
                  pl.BlockSpec((nkv, att_heads * 64, tk), lambda b, g: (b, g, 0))],
        out_specs=seq_blk(att_heads * 64),
        out_shape=jax.ShapeDtypeStruct((T, 512), BF16),
        scratch_shapes=att_scratch,
        compiler_params=_params(("arbitrary", "arbitrary")),
        name="attn_mla",
    )(q, k, vt)
    o_b = pl.pallas_call(
        functools.partial(_attn_kernel, fox=True, heads=att_heads, tiles=att_tiles, seq=S),
        grid=(B, n_grp),
        in_specs=[seq_blk(att_heads * 64), seq_blk(att_heads * 64), seq_all(LANES), seq_all(LANES),
                  pl.BlockSpec((nkv, att_heads * 64, tk), lambda b, g: (b, n_grp + g, 0))],
        out_specs=seq_blk(att_heads * 64),
        out_shape=jax.ShapeDtypeStruct((T, 512), BF16),
        scratch_shapes=att_scratch,
        compiler_params=_params(("arbitrary", "arbitrary")),
        name="attn_fox",
    )(fq, fk, qaug, kaug, vt)

    n_rt = 64
    wr = jnp.concatenate([w_router.T, w_group.T,
                          jnp.zeros((n_rt - N_EXPERTS - N_GROUPS, D), F32)], axis=0).astype(F32)
    wr_hi = wr.astype(BF16)
    wr_lo = (wr - wr_hi.astype(F32)).astype(BF16)
    br = jnp.concatenate([b_router.astype(F32), b_group.astype(F32),
                          jnp.full((n_rt - N_EXPERTS - N_GROUPS,), NEG_INF, F32)])[:, None]
    upper = (jnp.arange(tm)[:, None] < jnp.arange(tm)[None, :]).astype(BF16)
    h, hn, meta, wts, counts = pl.pallas_call(
        _post_kernel,
        grid=(n_tiles,),
        in_specs=[tok(D), tok(512), tok(512), tok(2 * D), full((512, D)), full((512, D)),
                  full((D, D)), full((1, D)), full((n_rt, D)), full((n_rt, D)), full((n_rt, 1)),
                  full((tm, tm))],
        out_specs=[tok(D), pl.BlockSpec((tm * ROW_CHUNKS, LANES), lambda i: (i, 0)),
                   pl.BlockSpec((1, 8, tm), lambda i: (i, 0, 0)), tok(LANES), full((N_EXPERTS, LANES))],
        out_shape=[jax.ShapeDtypeStruct((T, D), F32), jax.ShapeDtypeStruct((T * ROW_CHUNKS, LANES), F32),
                   jax.ShapeDtypeStruct((n_tiles, 8, tm), jnp.int32),
                   jax.ShapeDtypeStruct((T, LANES), F32),
                   jax.ShapeDtypeStruct((N_EXPERTS, LANES), F32)],
        scratch_shapes=[pltpu.VMEM((N_EXPERTS, LANES), F32)],
        compiler_params=_params(("arbitrary",)),
        name="post",
    )(x2, o_a, o_b, gates, w_o_mla.astype(BF16), w_o_fox.astype(BF16), w_out.astype(BF16),
      row(g_ffn), wr_hi, wr_lo, br, upper)

    n_tiles_e = (2 * T) // tm_e + N_EXPERTS
    cnt = counts[:, 0].astype(jnp.int32)
    tiles_per_e = (cnt + (tm_e - 1)) // tm_e
    cum_tiles = jnp.cumsum(tiles_per_e)
    off = ((cum_tiles - tiles_per_e) * tm_e).astype(jnp.int32)
    n_used = cum_tiles[-1:].astype(jnp.int32)
    tile_ids = jnp.minimum(jnp.arange(n_tiles_e, dtype=jnp.int32), n_used[0] - 1)
    tile_expert = jnp.sum(tile_ids[:, None] >= cum_tiles[None, :], axis=1).astype(jnp.int32)

    experts = jnp.arange(N_EXPERTS, dtype=jnp.int32)
    run_start = lambda e: jnp.sum(jnp.where(e[..., None] == experts, off, 0), axis=-1)
    pos = jnp.stack([run_start(meta[:, 0, :]) + meta[:, 2, :],
                     run_start(meta[:, 1, :]) + meta[:, 3, :]], axis=1).reshape(-1)

    any_spec = pl.BlockSpec(memory_space=pl.ANY)
    dma_sem = pltpu.SemaphoreType.DMA(())
    row_tile = (ROW_CHUNKS, LANES)
    n_rows_e = n_tiles_e * tm_e
    xs = pl.pallas_call(
        functools.partial(_dispatch_kernel, tm=tm, tm_e=tm_e, n_tiles_e=n_tiles_e),
        grid_spec=pltpu.PrefetchScalarGridSpec(
            num_scalar_prefetch=3, grid=(n_tiles,),
            in_specs=[any_spec, pl.BlockSpec((tm,) + row_tile, lambda i, *_: (i, 0, 0))],
            out_specs=any_spec,
            scratch_shapes=[pltpu.SMEM((2 * tm,), jnp.int32), pltpu.VMEM((tm_e,) + row_tile, F32),
                            dma_sem, dma_sem, dma_sem]),
        out_shape=jax.ShapeDtypeStruct((n_rows_e,) + row_tile, F32),
        compiler_params=_params(("arbitrary",)),
        name="dispatch",
    )(off, cnt, n_used, pos, hn.reshape((T,) + row_tile))

    used = lambda i, texp, nused: (jnp.minimum(i, nused[0] - 1), 0)
    ys = pl.pallas_call(
        _moe_kernel,
        grid_spec=pltpu.PrefetchScalarGridSpec(
            num_scalar_prefetch=2, grid=(n_tiles_e,),
            in_specs=[pl.BlockSpec((tm_e * ROW_CHUNKS, LANES), used),
                      pl.BlockSpec((1, D, EXPERT_FF), lambda i, texp, nused: (texp[i], 0, 0)),
                      pl.BlockSpec((1, D, EXPERT_FF), lambda i, texp, nused: (texp[i], 0, 0)),
                      pl.BlockSpec((1, EXPERT_FF, D), lambda i, texp, nused: (texp[i], 0, 0))],
            out_specs=pl.BlockSpec((tm_e * ROW_CHUNKS, LANES), lambda i, texp, nused: (i, 0))),
        out_shape=jax.ShapeDtypeStruct((n_rows_e * ROW_CHUNKS, LANES), F32),
        compiler_params=_params(("arbitrary",)),
        name="moe",
    )(tile_expert, n_used, xs.reshape(n_rows_e * ROW_CHUNKS, LANES), w_e_gate, w_e_up, w_e_down)

    out = pl.pallas_call(
        functools.partial(_combine_kernel, tm=tm),
        grid=(n_tiles,),
        in_specs=[any_spec, any_spec, tok(D), tok(LANES), full((1, D))],
        out_specs=tok(D),
        scratch_shapes=[pltpu.SMEM((2 * tm,), jnp.int32), pltpu.VMEM((tm * ROW_CHUNKS, LANES), F32),
                        pltpu.VMEM((tm * ROW_CHUNKS, LANES), F32), dma_sem, dma_sem],
        out_shape=jax.ShapeDtypeStruct((T, D), F32),
        compiler_params=_params(("arbitrary",)),
        name="combine",
    )(pos, ys.reshape((n_rows_e,) + row_tile), h, wts, row(g_final))
    return out.reshape(B, S, D)
```

```python
import functools

import jax
import jax.numpy as jnp
from jax import lax
from jax.experimental import pallas as pl
from jax.experimental.pallas import tpu as pltpu

F32 = jnp.float32
BF16 = jnp.bfloat16

D_MODEL = 1024
CHUNK = 64
EPS = 1e-6
NEG_INF = -1e30
LOG2E = 1.4426950408889634

MLA_HEADS = 8
MLA_NOPE_DIM = 64
MLA_ROPE_DIM = 32
MLA_V_DIM = 64
MLA_Q_RANK = 256
MLA_KV_RANK = 128
ROPE_THETA = 10000.0
FOX_HEADS = 8
FOX_HEAD_DIM = 64
N_GROUPS = 4
EXPERTS_PER_GROUP = 8
N_EXPERTS = 32
EXPERT_FF = 256

LANES = 128
HEAD_SLOT = LANES
AUG_SLOT = 8
ATT_TILE = 256
ACC_ROWS = 64 + 16
ROW_CHUNKS = D_MODEL // LANES
ROW_DMA_UNROLL = 16
VMEM_LIMIT_BYTES = 56 * 1024 * 1024

_SEG_WIDTHS = dict(cq=MLA_Q_RANK, ckv_flog=MLA_KV_RANK + LANES, kr=2 * HEAD_SLOT,
                   fq=512, fk=512, fv=512, ga=D_MODEL, gb=D_MODEL)
_SEG = {}
_off = 0
for _name, _w in _SEG_WIDTHS.items():
    _SEG[_name] = (_off, _off + _w)
    _off += _w
W1_COLS = _off


def _split3(v):
    hi = v.astype(BF16).astype(F32)
    r = v - hi
    mid = r.astype(BF16).astype(F32)
    lo = (r - mid).astype(BF16).astype(F32)
    return hi, mid, lo


def _proj_kernel(x_ref, pos_ref, gmix_ref, w1_ref, gcq_ref, gckv_ref, wqm_ref, wqs_ref, wuk_ref,
                 wuv_ref, bf_ref, freq_ref, tri_ref,
                 q_ref, k_ref, vt_ref, fq_ref, fk_ref, qaug_ref, kaug_ref, gates_ref,
                 carry_ref, *, tiles_per_seq, tk, scale_a, scale_b):
    i = pl.program_id(0)
    tm = x_ref.shape[0]

    x = x_ref[...]
    ms = jnp.mean(x * x, axis=-1, keepdims=True)
    y = ((x * lax.rsqrt(ms + EPS)) * gmix_ref[...]).astype(BF16)

    def seg(name):
        a, b = _SEG[name]
        return jnp.dot(y, w1_ref[:, a:b], preferred_element_type=F32)

    ang = pos_ref[...].astype(F32) * freq_ref[...]
    cos = jnp.cos(ang)
    sin = jnp.sin(ang)

    cq = seg("cq")
    cqn = ((cq * lax.rsqrt(jnp.mean(cq * cq, axis=-1, keepdims=True) + EPS)) * gcq_ref[...]).astype(BF16)
    qm = jnp.dot(cqn, wqm_ref[...], preferred_element_type=F32)
    qs = jnp.dot(cqn, wqs_ref[...], preferred_element_type=F32)
    cos_q = cos * (scale_a * LOG2E)
    sin_q = sin * (scale_a * LOG2E)
    ckv_flog = seg("ckv_flog")
    ckv = ckv_flog[:, 0:MLA_KV_RANK]
    ckvn = ((ckv * lax.rsqrt(jnp.mean(ckv * ckv, axis=-1, keepdims=True) + EPS)) * gckv_ref[...]).astype(BF16)
    knope = jnp.dot(ckvn, wuk_ref[...], preferred_element_type=F32)
    va = jnp.dot(ckvn, wuv_ref[...], preferred_element_type=F32)
    kr = seg("kr")
    krope = kr[:, 0:HEAD_SLOT] * cos + kr[:, HEAD_SLOT:2 * HEAD_SLOT] * sin
    for h in range(MLA_HEADS):
        sl = slice(h * HEAD_SLOT, (h + 1) * HEAD_SLOT)
        q_ref[:, sl] = (qm[:, sl] * cos_q + qs[:, sl] * sin_q).astype(BF16)
        k_ref[:, sl] = (knope[:, sl] + krope).astype(BF16)

    fq_ref[...] = (seg("fq") * (scale_b * LOG2E)).astype(BF16)
    fk_ref[...] = seg("fk").astype(BF16)
    fv = seg("fv")
    va_t = va.T
    fv_t = fv.T
    for c in range(tm // tk):
        vt_ref[c, 0:512, :] = va_t[:, c * tk:(c + 1) * tk].astype(BF16)
        vt_ref[c, 512:1024, :] = fv_t[:, c * tk:(c + 1) * tk].astype(BF16)

    logf = jax.nn.log_sigmoid(ckv_flog[:, MLA_KV_RANK:MLA_KV_RANK + LANES] + bf_ref[...])
    hi, mid, lo = _split3(logf)
    tri = tri_ref[...]
    cum = (jnp.dot(tri, hi.astype(BF16), preferred_element_type=F32)
           + jnp.dot(tri, mid.astype(BF16), preferred_element_type=F32)
           + jnp.dot(tri, lo.astype(BF16), preferred_element_type=F32))

    @pl.when(i % tiles_per_seq == 0)
    def _():
        carry_ref[...] = jnp.zeros_like(carry_ref)

    decay = cum + carry_ref[0:1, :]
    carry_ref[0:1, :] = decay[tm - 1:tm, :]
    d_hi, d_mid, d_lo = _split3(decay * LOG2E)
    lane = lax.broadcasted_iota(jnp.int32, (1, LANES), 1)
    l8 = lane % AUG_SLOT
    valid = lane < FOX_HEADS * AUG_SLOT
    one = jnp.ones_like(d_hi)
    zero = jnp.zeros_like(d_hi)
    qaug = jnp.where(l8 == 0, d_hi, jnp.where(l8 == 1, d_mid, jnp.where(l8 == 2, d_lo,
                     jnp.where(l8 < 6, one, zero))))
    kaug = jnp.where(l8 < 3, one, jnp.where(l8 == 3, -d_hi, jnp.where(l8 == 4, -d_mid,
                     jnp.where(l8 == 5, -d_lo, zero))))
    qaug_ref[...] = jnp.where(valid, qaug, zero).astype(BF16)
    kaug_ref[...] = jnp.where(valid, kaug, zero).astype(BF16)

    gates_ref[:, 0:D_MODEL] = jax.nn.sigmoid(seg("ga")).astype(BF16)
    gates_ref[:, D_MODEL:2 * D_MODEL] = jax.nn.sigmoid(seg("gb")).astype(BF16)


def _attn_kernel(*refs, fox, heads, tiles, seq):
    if fox:
        q_ref, k_ref, qaug_ref, kaug_ref, vt_ref, o_ref, m_scr, acc_scr, s_scr = refs
    else:
        q_ref, k_ref, vt_ref, o_ref, m_scr, acc_scr, s_scr = refs
    assert tiles % 2 == 0
    grp = pl.program_id(1)
    tq = tiles * ATT_TILE
    nq = seq // tq
    lane = lax.broadcasted_iota(jnp.int32, (1, LANES), 1)
    nt = (((1,), (1,)), ((), ()))
    chains = [(h, t) for h in range(heads) for t in range(tiles)]

    def q_block(i, _):
        q0 = pl.multiple_of(i * tq, tq)
        qops = {}
        for h in range(heads):
            for t in range(tiles):
                rows = pl.ds(pl.multiple_of(q0 + t * ATT_TILE, ATT_TILE), ATT_TILE)
                if fox:
                    blk = slice((h // 2) * LANES, (h // 2 + 1) * LANES)
                    qp = q_ref[rows, blk]
                    qa = qaug_ref[rows, :]
                    zq = jnp.zeros_like(qp)
                    qh = jnp.where(lane // FOX_HEAD_DIM == h % 2, qp, zq)
                    qah = jnp.where(lane // AUG_SLOT == heads * grp + h, qa, zq)
                    qops[(h, t)] = jnp.concatenate([qh, qah], axis=1)
                else:
                    qops[(h, t)] = q_ref[rows, h * HEAD_SLOT:(h + 1) * HEAD_SLOT]
        for c in range(len(chains)):
            m_scr[c] = jnp.full((8, ATT_TILE), NEG_INF, F32)
            acc_scr[c] = jnp.zeros((ACC_ROWS, ATT_TILE), F32)
        ones_rows = jnp.ones((ACC_ROWS - 64, ATT_TILE), BF16)

        def logits(j, slot, min_tile):
            krows = pl.ds(pl.multiple_of(j * ATT_TILE, ATT_TILE), ATT_TILE)
            kops = {}
            for h in range(heads):
                if fox:
                    if h % 2 == 0:
                        blk = slice((h // 2) * LANES, (h // 2 + 1) * LANES)
                        kops[h] = jnp.concatenate([k_ref[krows, blk], kaug_ref[krows, :]], axis=1)
                    else:
                        kops[h] = kops[h - 1]
                else:
                    kops[h] = k_ref[krows, h * HEAD_SLOT:(h + 1) * HEAD_SLOT]
            for c, ht in enumerate(chains):
                if ht[1] >= min_tile:
                    s_scr[slot, c] = lax.dot_general(kops[ht[0]], qops[ht], nt,
                                                     preferred_element_type=F32)

        def consume(j, slot, diag):
            active = [(c, ht) for c, ht in enumerate(chains) if diag is None or ht[1] >= diag]
            if diag is not None:
                kpos = lax.broadcasted_iota(jnp.int32, (ATT_TILE, 1), 0)
                qpos = lax.broadcasted_iota(jnp.int32, (1, ATT_TILE), 1)
                allowed = (kpos <= qpos) if fox else ((kpos // CHUNK) <= (qpos // CHUNK))
            p_all, alpha_all = {}, {}
            for c, ht in active:
                s = s_scr[slot, c]
                if diag is not None and ht[1] == diag:
                    s = jnp.where(allowed, s, NEG_INF)
                m = m_scr[c][0:1, :]
                m_new = jnp.maximum(m, jnp.max(s, axis=0, keepdims=True))
                alpha_all[c] = jnp.exp2(m - m_new)
                p_all[c] = jnp.exp2(s - m_new).astype(BF16)
                m_scr[c] = jnp.broadcast_to(m_new, (8, ATT_TILE))
            for c, ht in active:
                vt = jnp.concatenate([vt_ref[j, ht[0] * 64:(ht[0] + 1) * 64, :], ones_rows], axis=0)
                pv = jnp.dot(vt, p_all[c], preferred_element_type=F32)
                acc_scr[c] = alpha_all[c] * acc_scr[c] + pv

        n_full = i * tiles
        logits(0, 0, 0)

        def two_steps(u, _):
            j = 2 * u
            logits(j + 1, 1, 0)
            consume(j, 0, None)
            logits(j + 2, 0, 0)
            consume(j + 1, 1, None)
            return 0

        lax.fori_loop(0, n_full // 2, two_steps, 0)
        for d in range(tiles):
            if d + 1 < tiles:
                logits(n_full + d + 1, (d + 1) % 2, d + 1)
            consume(n_full + d, d % 2, d)
        for t in range(tiles):
            rows = pl.ds(pl.multiple_of(q0 + t * ATT_TILE, ATT_TILE), ATT_TILE)
            for hp in range(heads // 2):
                parts = []
                for h in (2 * hp, 2 * hp + 1):
                    acc = acc_scr[chains.index((h, t))]
                    parts.append(acc[0:64, :] / acc[64:65, :])
                o_t = jnp.concatenate(parts, axis=0)
                o_ref[rows, hp * LANES:(hp + 1) * LANES] = o_t.T.astype(BF16)
        return 0

    lax.fori_loop(0, nq, q_block, 0)


def _post_kernel(x_ref, oa_ref, ob_ref, gates_ref, woa_ref, wob_ref, wout_ref, gffn_ref,
                 wrh_ref, wrl_ref, br_ref, upper_ref, h_ref, hn_ref, meta_ref, wts_ref, cnt_ref, cnt_scr):
    tm = x_ref.shape[0]
    ya = jnp.dot(oa_ref[...], woa_ref[...], preferred_element_type=F32)
    yb = jnp.dot(ob_ref[...], wob_ref[...], preferred_element_type=F32)
    sa = gates_ref[:, 0:D_MODEL].astype(F32)
    sb = gates_ref[:, D_MODEL:2 * D_MODEL].astype(F32)
    mixed = (sa * ya + sb * yb).astype(BF16)
    h = x_ref[...] + jnp.dot(mixed, wout_ref[...], preferred_element_type=F32)
    h_ref[...] = h
    hn = (h * lax.rsqrt(jnp.mean(h * h, axis=-1, keepdims=True) + EPS)) * gffn_ref[...]
    hn_hi = hn.astype(BF16)
    hn_lo = (hn - hn_hi.astype(F32)).astype(BF16)
    for s in range(ROW_CHUNKS):
        hn_ref[pl.ds(s, tm, stride=ROW_CHUNKS), :] = hn[:, s * LANES:(s + 1) * LANES]

    nt = (((1,), (1,)), ((), ()))
    wrh = wrh_ref[...]
    lt = (lax.dot_general(wrh, hn_hi, nt, preferred_element_type=F32)
          + lax.dot_general(wrh, hn_lo, nt, preferred_element_type=F32)
          + lax.dot_general(wrl_ref[...], hn_hi, nt, preferred_element_type=F32)) + br_ref[...]

    row8 = lax.broadcasted_iota(jnp.int32, (EXPERTS_PER_GROUP, tm), 0).astype(F32)
    big = jnp.float32(EXPERTS_PER_GROUP)
    lg = lt[N_EXPERTS:N_EXPERTS + 8, :]
    gmax = jnp.max(lg, axis=0, keepdims=True)
    g_p = 1.0 / jnp.sum(jnp.exp(lg - gmax), axis=0, keepdims=True)
    g_idx = jnp.min(jnp.where(lg == gmax, row8, big), axis=0, keepdims=True)
    sel = jnp.zeros((EXPERTS_PER_GROUP, tm), F32)
    for g in range(N_GROUPS):
        sel = jnp.where(g_idx == g, lt[g * EXPERTS_PER_GROUP:(g + 1) * EXPERTS_PER_GROUP, :], sel)
    m1 = jnp.max(sel, axis=0, keepdims=True)
    i1 = jnp.min(jnp.where(sel == m1, row8, big), axis=0, keepdims=True)
    sel2 = jnp.where(row8 == i1, -jnp.inf, sel)
    m2 = jnp.max(sel2, axis=0, keepdims=True)
    i2 = jnp.min(jnp.where(sel2 == m2, row8, big), axis=0, keepdims=True)
    r = jnp.exp(m2 - m1)
    w1 = g_p / (1.0 + r)
    w2 = g_p * r / (1.0 + r)
    e1 = g_idx * EXPERTS_PER_GROUP + i1
    e2 = g_idx * EXPERTS_PER_GROUP + i2
    @pl.when(pl.program_id(0) == 0)
    def _():
        cnt_scr[...] = jnp.zeros_like(cnt_scr)

    rowe = lax.broadcasted_iota(jnp.int32, (N_EXPERTS, tm), 0).astype(F32)
    is1 = rowe == e1
    is2 = rowe == e2
    onehot = jnp.where(is1 | is2, 1.0, 0.0)
    before = jnp.dot(onehot.astype(BF16), upper_ref[...], preferred_element_type=F32) + cnt_scr[:, 0:1]
    r1 = jnp.sum(jnp.where(is1, before, 0.0), axis=0, keepdims=True)
    r2 = jnp.sum(jnp.where(is2, before, 0.0), axis=0, keepdims=True)
    cnt_scr[...] = cnt_scr[...] + jnp.sum(onehot, axis=1, keepdims=True)
    cnt_ref[...] = cnt_scr[...]
    zrow = jnp.zeros_like(e1)
    meta_ref[0] = jnp.concatenate([e1, e2, r1, r2, zrow, zrow, zrow, zrow], axis=0).astype(jnp.int32)
    row = lax.broadcasted_iota(jnp.int32, (LANES, tm), 0)
    wts_t = jnp.where(row == 0, w1, jnp.where(row == 1, w2, 0.0))
    wts_ref[...] = wts_t.T


def _dispatch_kernel(off_ref, cnt_ref, nused_ref, pos_hbm, hn_ref, xs_hbm, pos_smem, zbuf, sem_meta,
                     sem_zero, sem_rows, *, tm, tm_e, n_tiles_e):
    i = pl.program_id(0)
    meta_cp = pltpu.make_async_copy(pos_hbm.at[pl.ds(pl.multiple_of(i * 2 * tm, 2 * tm), 2 * tm)],
                                    pos_smem, sem_meta)
    meta_cp.start()

    @pl.when(i == 0)
    def _():
        zbuf[...] = jnp.zeros_like(zbuf)
        shift = tm_e.bit_length() - 1

        def unused_tile(t, _):
            cp = pltpu.make_async_copy(zbuf, xs_hbm.at[pl.ds(pl.multiple_of(t * tm_e, tm_e), tm_e)], sem_zero)
            cp.start()
            cp.wait()
            return 0

        lax.fori_loop(nused_ref[0], n_tiles_e, unused_tile, 0)

        def tail_copy(e):
            n = cnt_ref[e]
            start = off_ref[e] + lax.shift_left(lax.shift_right_logical(jnp.maximum(n, 1) - 1, shift), shift)
            return n, pltpu.make_async_copy(zbuf, xs_hbm.at[pl.ds(pl.multiple_of(start, tm_e), tm_e)], sem_zero)

        for e in range(N_EXPERTS):
            n, cp = tail_copy(e)

            @pl.when(n > 0)
            def _():
                cp.start()
        for e in range(N_EXPERTS):
            n, cp = tail_copy(e)

            @pl.when(n > 0)
            def _():
                cp.wait()

    meta_cp.wait()

    def row_copies(t):
        src = hn_ref.at[t]
        return (pltpu.make_async_copy(src, xs_hbm.at[pos_smem[t]], sem_rows),
                pltpu.make_async_copy(src, xs_hbm.at[pos_smem[tm + t]], sem_rows))

    def issue(t, _):
        for prio, cp in enumerate(row_copies(t)):
            cp.start(priority=prio)
        return 0

    def drain(t, _):
        for cp in row_copies(t):
            cp.wait()
        return 0

    lax.fori_loop(0, tm, issue, 0, unroll=ROW_DMA_UNROLL)
    lax.fori_loop(0, tm, drain, 0, unroll=ROW_DMA_UNROLL)


def _moe_kernel(texp_ref, nused_ref, xs_ref, wg_ref, wu_ref, wd_ref, ys_ref):
    del texp_ref

    @pl.when(pl.program_id(0) >= nused_ref[0])
    def _():
        ys_ref[...] = jnp.zeros_like(ys_ref)

    @pl.when(pl.program_id(0) < nused_ref[0])
    def _():
        tm_e = xs_ref.shape[0] // ROW_CHUNKS
        x = jnp.concatenate([xs_ref[pl.ds(s, tm_e, stride=ROW_CHUNKS), :] for s in range(ROW_CHUNKS)],
                            axis=1).astype(BF16)
        g = jnp.dot(x, wg_ref[0].astype(BF16), preferred_element_type=F32)
        u = jnp.dot(x, wu_ref[0].astype(BF16), preferred_element_type=F32)
        a = (jax.nn.silu(g) * u).astype(BF16)
        y = jnp.dot(a, wd_ref[0].astype(BF16), preferred_element_type=F32)
        for s in range(ROW_CHUNKS):
            ys_ref[pl.ds(s, tm_e, stride=ROW_CHUNKS), :] = y[:, s * LANES:(s + 1) * LANES]


def _combine_kernel(pos_hbm, ys_hbm, h_ref, wts_ref, gfin_ref, out_ref, pos_smem, y1_buf, y2_buf,
                    sem_meta, sem_rows, *, tm):
    i = pl.program_id(0)
    meta_cp = pltpu.make_async_copy(pos_hbm.at[pl.ds(pl.multiple_of(i * 2 * tm, 2 * tm), 2 * tm)],
                                    pos_smem, sem_meta)
    meta_cp.start()
    meta_cp.wait()

    def row_copies(t):
        dst = pl.ds(pl.multiple_of(t * ROW_CHUNKS, ROW_CHUNKS), ROW_CHUNKS)
        return (pltpu.make_async_copy(ys_hbm.at[pos_smem[t]], y1_buf.at[dst], sem_rows),
                pltpu.make_async_copy(ys_hbm.at[pos_smem[tm + t]], y2_buf.at[dst], sem_rows))

    def issue(t, _):
        for prio, cp in enumerate(row_copies(t)):
            cp.start(priority=prio)
        return 0

    def drain(t, _):
        for cp in row_copies(t):
            cp.wait()
        return 0

    lax.fori_loop(0, tm, issue, 0, unroll=ROW_DMA_UNROLL)
    lax.fori_loop(0, tm, drain, 0, unroll=ROW_DMA_UNROLL)
    w = wts_ref[...]
    w1 = w[:, 0:1]
    w2 = w[:, 1:2]
    y = jnp.concatenate([w1 * y1_buf[pl.ds(s, tm, stride=ROW_CHUNKS), :]
                         + w2 * y2_buf[pl.ds(s, tm, stride=ROW_CHUNKS), :] for s in range(ROW_CHUNKS)], axis=1)
    hh = h_ref[...] + y
    out_ref[...] = (hh * lax.rsqrt(jnp.mean(hh * hh, axis=-1, keepdims=True) + EPS)) * gfin_ref[...]


def _pad_heads(w, heads, dim):
    k = w.shape[0]
    w = w.reshape(k, heads, dim)
    return jnp.pad(w, ((0, 0), (0, 0), (0, HEAD_SLOT - dim))).reshape(k, heads * HEAD_SLOT)


def _swap_rope(w_rope):
    half = MLA_ROPE_DIM // 2
    return jnp.concatenate([-w_rope[..., half:], w_rope[..., :half]], axis=-1)


def _params(sem):
    return pltpu.CompilerParams(dimension_semantics=sem, vmem_limit_bytes=VMEM_LIMIT_BYTES)


def kernel(x, positions, g_mix, w_in, b_forget, g_cq, g_ckv, w_uq, w_uk, w_uv, w_o_mla, w_o_fox,
           w_out, g_ffn, w_group, b_group, w_router, b_router, w_e_gate, w_e_up, w_e_down, g_final):
    B, S, D = x.shape
    T = B * S
    assert D == D_MODEL
    tm = min(512, S)
    tk = ATT_TILE
    att_heads = 4
    att_tiles = 4
    tm_e = 512
    assert S % tm == 0 and S % (att_tiles * ATT_TILE) == 0 and tm % tk == 0 and tk % CHUNK == 0

    cq_w, ckv_w, kr_w, fq_w, fk_w, fv_w, fl_w, ga_w, gb_w = jnp.split(
        w_in, [256, 384, 416, 928, 1440, 1952, 1960, 2984], axis=1)
    z64 = jnp.zeros((D, MLA_NOPE_DIM), F32)
    z32 = jnp.zeros((D, HEAD_SLOT - MLA_NOPE_DIM - MLA_ROPE_DIM), F32)
    fl_rep = jnp.pad(jnp.repeat(fl_w, AUG_SLOT, axis=1), ((0, 0), (0, LANES - FOX_HEADS * AUG_SLOT)))
    w1 = jnp.concatenate([cq_w, ckv_w, fl_rep,
                          jnp.concatenate([z64, kr_w, z32], axis=1),
                          jnp.concatenate([z64, _swap_rope(kr_w), z32], axis=1),
                          fq_w, fk_w, fv_w, ga_w, gb_w], axis=1).astype(BF16)
    assert w1.shape[1] == W1_COLS
    bf128 = jnp.pad(jnp.repeat(b_forget.astype(F32), AUG_SLOT), (0, LANES - FOX_HEADS * AUG_SLOT))[None, :]
    dq = MLA_NOPE_DIM + MLA_ROPE_DIM
    wq3 = w_uq.reshape(MLA_Q_RANK, MLA_HEADS, dq)
    wq_main = _pad_heads(w_uq, MLA_HEADS, dq).astype(BF16)
    wq_swap3 = jnp.concatenate([jnp.zeros((MLA_Q_RANK, MLA_HEADS, MLA_NOPE_DIM), F32),
                                _swap_rope(wq3[..., MLA_NOPE_DIM:])], axis=-1)
    wq_swap = _pad_heads(wq_swap3.reshape(MLA_Q_RANK, MLA_HEADS * dq), MLA_HEADS, dq).astype(BF16)
    wuk_pad = _pad_heads(w_uk, MLA_HEADS, MLA_NOPE_DIM).astype(BF16)
    half = MLA_ROPE_DIM // 2
    inv_freq = ROPE_THETA ** (-jnp.arange(half, dtype=F32) / half)
    freq128 = jnp.concatenate([jnp.zeros((MLA_NOPE_DIM,), F32), inv_freq, inv_freq,
                               jnp.zeros((HEAD_SLOT - dq,), F32)])[None, :]
    tri = (jnp.arange(tm)[:, None] >= jnp.arange(tm)[None, :]).astype(BF16)

    x2 = x.reshape(T, D)
    pos2 = positions.reshape(T, 1).astype(jnp.int32)
    row = lambda v: v.astype(F32)[None, :]

    full = lambda shape: pl.BlockSpec(shape, lambda i: (0,) * len(shape))
    tok = lambda w: pl.BlockSpec((tm, w), lambda i: (i, 0))
    n_tiles = T // tm
    nkv = S // tk

    q, k, vt, fq, fk, qaug, kaug, gates = pl.pallas_call(
        functools.partial(_proj_kernel, tiles_per_seq=S // tm, tk=tk,
                          scale_a=float(dq) ** -0.5, scale_b=float(FOX_HEAD_DIM) ** -0.5),
        grid=(n_tiles,),
        in_specs=[tok(D), tok(1), full((1, D)), full((D, W1_COLS)), full((1, MLA_Q_RANK)),
                  full((1, MLA_KV_RANK)), full((MLA_Q_RANK, MLA_HEADS * HEAD_SLOT)),
                  full((MLA_Q_RANK, MLA_HEADS * HEAD_SLOT)), full((MLA_KV_RANK, MLA_HEADS * HEAD_SLOT)),
                  full((MLA_KV_RANK, 512)), full((1, LANES)), full((1, LANES)), full((tm, tm))],
        out_specs=[tok(1024), tok(1024),
                   pl.BlockSpec((tm // tk, 1024, tk), lambda i: (i, 0, 0)),
                   tok(512), tok(512), tok(LANES), tok(LANES), tok(2 * D)],
        out_shape=[jax.ShapeDtypeStruct((T, 1024), BF16), jax.ShapeDtypeStruct((T, 1024), BF16),
                   jax.ShapeDtypeStruct((T // tk, 1024, tk), BF16),
                   jax.ShapeDtypeStruct((T, 512), BF16), jax.ShapeDtypeStruct((T, 512), BF16),
                   jax.ShapeDtypeStruct((T, LANES), BF16), jax.ShapeDtypeStruct((T, LANES), BF16),
                   jax.ShapeDtypeStruct((T, 2 * D), BF16)],
        scratch_shapes=[pltpu.VMEM((8, LANES), F32)],
        compiler_params=_params(("arbitrary",)),
        name="proj",
    )(x2, pos2, row(g_mix), w1, row(g_cq), row(g_ckv), wq_main, wq_swap, wuk_pad,
      w_uv.astype(BF16), bf128, freq128, tri)

    n_grp = MLA_HEADS // att_heads
    n_chains = att_heads * att_tiles
    seq_blk = lambda w: pl.BlockSpec((S, w), lambda b, g: (b, g))
    seq_all = lambda w: pl.BlockSpec((S, w), lambda b, g: (b, 0))
    att_scratch = [pltpu.VMEM((n_chains, 8, ATT_TILE), F32), pltpu.VMEM((n_chains, ACC_ROWS, ATT_TILE), F32),
                   pltpu.VMEM((2, n_chains, ATT_TILE, ATT_TILE), F32)]
    o_a = pl.pallas_call(
        functools.partial(_attn_kernel, fox=False, heads=att_heads, tiles=att_tiles, seq=S),
        grid=(B, n_grp),
        in_specs=[seq_blk(att_heads * HEAD_SLOT), seq_blk(att_heads * HEAD_SLOT),
                  pl.BlockSpec((nkv, att_heads * 64, tk), lambda b, g: (b, g, 0))],
        out_specs=seq_blk(att_heads * 64),
        out_shape=jax.ShapeDtypeStruct((T, 512), BF16),
        scratch_shapes=att_scratch,
        compiler_params=_params(("arbitrary", "arbitrary")),
        name="attn_mla",
    )(q, k, vt)
    o_b = pl.pallas_call(
        functools.partial(_attn_kernel, fox=True, heads=att_heads, tiles=att_tiles, seq=S),
        grid=(B, n_grp),
        in_specs=[seq_blk(att_heads * 64), seq_blk(att_heads * 64), seq_all(LANES), seq_all(LANES),
                  pl.BlockSpec((nkv, att_heads * 64, tk), lambda b, g: (b, n_grp + g, 0))],
        out_specs=seq_blk(att_heads * 64),
        out_shape=jax.ShapeDtypeStruct((T, 512), BF16),
        scratch_shapes=att_scratch,
        compiler_params=_params(("arbitrary", "arbitrary")),
        name="attn_fox",
    )(fq, fk, qaug, kaug, vt)

    n_rt = 64
    wr = jnp.concatenate([w_router.T, w_group.T,
                          jnp.zeros((n_rt - N_EXPERTS - N_GROUPS, D), F32)], axis=0).astype(F32)
    wr_hi = wr.astype(BF16)
    wr_lo = (wr - wr_hi.astype(F32)).astype(BF16)
    br = jnp.concatenate([b_router.astype(F32), b_group.astype(F32),
                          jnp.full((n_rt - N_EXPERTS - N_GROUPS,), NEG_INF, F32)])[:, None]
    upper = (jnp.arange(tm)[:, None] < jnp.arange(tm)[None, :]).astype(BF16)
    h, hn, meta, wts, counts = pl.pallas_call(
        _post_kernel,
        grid=(n_tiles,),
        in_specs=[tok(D), tok(512), tok(512), tok(2 * D), full((512, D)), full((512, D)),
                  full((D, D)), full((1, D)), full((n_rt, D)), full((n_rt, D)), full((n_rt, 1)),
                  full((tm, tm))],
        out_specs=[tok(D), pl.BlockSpec((tm * ROW_CHUNKS, LANES), lambda i: (i, 0)),
                   pl.BlockSpec((1, 8, tm), lambda i: (i, 0, 0)), tok(LANES), full((N_EXPERTS, LANES))],
        out_shape=[jax.ShapeDtypeStruct((T, D), F32), jax.ShapeDtypeStruct((T * ROW_CHUNKS, LANES), F32),
                   jax.ShapeDtypeStruct((n_tiles, 8, tm), jnp.int32),
                   jax.ShapeDtypeStruct((T, LANES), F32),
                   jax.ShapeDtypeStruct((N_EXPERTS, LANES), F32)],
        scratch_shapes=[pltpu.VMEM((N_EXPERTS, LANES), F32)],
        compiler_params=_params(("arbitrary",)),
        name="post",
    )(x2, o_a, o_b, gates, w_o_mla.astype(BF16), w_o_fox.astype(BF16), w_out.astype(BF16),
      row(g_ffn), wr_hi, wr_lo, br, upper)

    n_tiles_e = (2 * T) // tm_e + N_EXPERTS
    cnt = counts[:, 0].astype(jnp.int32)
    tiles_per_e = (cnt + (tm_e - 1)) // tm_e
    cum_tiles = jnp.cumsum(tiles_per_e)
    off = ((cum_tiles - tiles_per_e) * tm_e).astype(jnp.int32)
    n_used = cum_tiles[-1:].astype(jnp.int32)
    tile_ids = jnp.minimum(jnp.arange(n_tiles_e, dtype=jnp.int32), n_used[0] - 1)
    tile_expert = jnp.sum(tile_ids[:, None] >= cum_tiles[None, :], axis=1).astype(jnp.int32)

    experts = jnp.arange(N_EXPERTS, dtype=jnp.int32)
    run_start = lambda e: jnp.sum(jnp.where(e[..., None] == experts, off, 0), axis=-1)
    pos = jnp.stack([run_start(meta[:, 0, :]) + meta[:, 2, :],
                     run_start(meta[:, 1, :]) + meta[:, 3, :]], axis=1).reshape(-1)

    any_spec = pl.BlockSpec(memory_space=pl.ANY)
    dma_sem = pltpu.SemaphoreType.DMA(())
    row_tile = (ROW_CHUNKS, LANES)
    n_rows_e = n_tiles_e * tm_e
    xs = pl.pallas_call(
        functools.partial(_dispatch_kernel, tm=tm, tm_e=tm_e, n_tiles_e=n_tiles_e),
        grid_spec=pltpu.PrefetchScalarGridSpec(
            num_scalar_prefetch=3, grid=(n_tiles,),
            in_specs=[any_spec, pl.BlockSpec((tm,) + row_tile, lambda i, *_: (i, 0, 0))],
            out_specs=any_spec,
            scratch_shapes=[pltpu.SMEM((2 * tm,), jnp.int32), pltpu.VMEM((tm_e,) + row_tile, F32),
                            dma_sem, dma_sem, dma_sem]),
        out_shape=jax.ShapeDtypeStruct((n_rows_e,) + row_tile, F32),
        compiler_params=_params(("arbitrary",)),
        name="dispatch",
    )(off, cnt, n_used, pos, hn.reshape((T,) + row_tile))

    used = lambda i, texp, nused: (jnp.minimum(i, nused[0] - 1), 0)
    ys = pl.pallas_call(
        _moe_kernel,
        grid_spec=pltpu.PrefetchScalarGridSpec(
            num_scalar_prefetch=2, grid=(n_tiles_e,),
            in_specs=[pl.BlockSpec((tm_e * ROW_CHUNKS, LANES), used),
                      pl.BlockSpec((1, D, EXPERT_FF), lambda i, texp, nused: (texp[i], 0, 0)),
                      pl.BlockSpec((1, D, EXPERT_FF), lambda i, texp, nused: (texp[i], 0, 0)),
                      pl.BlockSpec((1, EXPERT_FF, D), lambda i, texp, nused: (texp[i], 0, 0))],
            out_specs=pl.BlockSpec((tm_e * ROW_CHUNKS, LANES), lambda i, texp, nused: (i, 0))),
        out_shape=jax.ShapeDtypeStruct((n_rows_e * ROW_CHUNKS, LANES), F32),
        compiler_params=_params(("arbitrary",)),
        name="moe",
    )(tile_expert, n_used, xs.reshape(n_rows_e * ROW_CHUNKS, LANES), w_e_gate, w_e_up, w_e_down)

    out = pl.pallas_call(
        functools.partial(_combine_kernel, tm=tm),
        grid=(n_tiles,),
        in_specs=[any_spec, any_spec, tok(D), tok(LANES), full((1, D))],
        out_specs=tok(D),
        scratch_shapes=[pltpu.SMEM((2 * tm,), jnp.int32), pltpu.VMEM((tm * ROW_CHUNKS, LANES), F32),
                        pltpu.VMEM((tm * ROW_CHUNKS, LANES), F32), dma_sem, dma_sem],
        out_shape=jax.ShapeDtypeStruct((T, D), F32),
        compiler_params=_params(("arbitrary",)),
        name="combine",
    )(pos, ys.reshape((n_rows_e,) + row_tile), h, wts, row(g_final))
    return out.reshape(B, S, D)
```

```python
import functools

import jax
import jax.numpy as jnp
from jax import lax
from jax.experimental import pallas as pl
from jax.experimental.pallas import tpu as pltpu

F32 = jnp.float32
BF16 = jnp.bfloat16

D_MODEL = 1024
CHUNK = 64
EPS = 1e-6
NEG_INF = -1e30
LOG2E = 1.4426950408889634

MLA_HEADS = 8
MLA_NOPE_DIM = 64
MLA_ROPE_DIM = 32
MLA_V_DIM = 64
MLA_Q_RANK = 256
MLA_KV_RANK = 128
ROPE_THETA = 10000.0
FOX_HEADS = 8
FOX_HEAD_DIM = 64
N_GROUPS = 4
EXPERTS_PER_GROUP = 8
N_EXPERTS = 32
EXPERT_FF = 256

LANES = 128
HEAD_SLOT = LANES
AUG_SLOT = 8
ATT_TILE = 256
ACC_ROWS = 64 + 16
ROW_CHUNKS = D_MODEL // LANES
ROW_DMA_UNROLL = 16
VMEM_LIMIT_BYTES = 56 * 1024 * 1024

_SEG_WIDTHS = dict(cq=MLA_Q_RANK, ckv_flog=MLA_KV_RANK + LANES, kr=2 * HEAD_SLOT,
                   fq=512, fk=512, fv=512, ga=D_MODEL, gb=D_MODEL)
_SEG = {}
_off = 0
for _name, _w in _SEG_WIDTHS.items():
    _SEG[_name] = (_off, _off + _w)
    _off += _w
W1_COLS = _off


def _split3(v):
    hi = v.astype(BF16).astype(F32)
    r = v - hi
    mid = r.astype(BF16).astype(F32)
    lo = (r - mid).astype(BF16).astype(F32)
    return hi, mid, lo


def _proj_kernel(x_ref, pos_ref, gmix_ref, w1_ref, gcq_ref, gckv_ref, wqm_ref, wqs_ref, wuk_ref,
                 wuv_ref, bf_ref, freq_ref, tri_ref,
                 q_ref, k_ref, vt_ref, fq_ref, fk_ref, qaug_ref, kaug_ref, gates_ref,
                 carry_ref, *, tiles_per_seq, tk, scale_a, scale_b):
    i = pl.program_id(0)
    tm = x_ref.shape[0]

    x = x_ref[...]
    ms = jnp.mean(x * x, axis=-1, keepdims=True)
    y = ((x * lax.rsqrt(ms + EPS)) * gmix_ref[...]).astype(BF16)

    def seg(name):
        a, b = _SEG[name]
        return jnp.dot(y, w1_ref[:, a:b], preferred_element_type=F32)

    ang = pos_ref[...].astype(F32) * freq_ref[...]
    cos = jnp.cos(ang)
    sin = jnp.sin(ang)

    cq = seg("cq")
    cqn = ((cq * lax.rsqrt(jnp.mean(cq * cq, axis=-1, keepdims=True) + EPS)) * gcq_ref[...]).astype(BF16)
    qm = jnp.dot(cqn, wqm_ref[...], preferred_element_type=F32)
    qs = jnp.dot(cqn, wqs_ref[...], preferred_element_type=F32)
    cos_q = cos * (scale_a * LOG2E)
    sin_q = sin * (scale_a * LOG2E)
    ckv_flog = seg("ckv_flog")
    ckv = ckv_flog[:, 0:MLA_KV_RANK]
    ckvn = ((ckv * lax.rsqrt(jnp.mean(ckv * ckv, axis=-1, keepdims=True) + EPS)) * gckv_ref[...]).astype(BF16)
    knope = jnp.dot(ckvn, wuk_ref[...], preferred_element_type=F32)
    va = jnp.dot(ckvn, wuv_ref[...], preferred_element_type=F32)
    kr = seg("kr")
    krope = kr[:, 0:HEAD_SLOT] * cos + kr[:, HEAD_SLOT:2 * HEAD_SLOT] * sin
    for h in range(MLA_HEADS):
        sl = slice(h * HEAD_SLOT, (h + 1) * HEAD_SLOT)
        q_ref[:, sl] = (qm[:, sl] * cos_q + qs[:, sl] * sin_q).astype(BF16)
        k_ref[:, sl] = (knope[:, sl] + krope).astype(BF16)

    fq_ref[...] = (seg("fq") * (scale_b * LOG2E)).astype(BF16)
    fk_ref[...] = seg("fk").astype(BF16)
    fv = seg("fv")
    va_t = va.T
    fv_t = fv.T
    for c in range(tm // tk):
        vt_ref[c, 0:512, :] = va_t[:, c * tk:(c + 1) * tk].astype(BF16)
        vt_ref[c, 512:1024, :] = fv_t[:, c * tk:(c + 1) * tk].astype(BF16)

    logf = jax.nn.log_sigmoid(ckv_flog[:, MLA_KV_RANK:MLA_KV_RANK + LANES] + bf_ref[...])
    hi, mid, lo = _split3(logf)
    tri = tri_ref[...]
    cum = (jnp.dot(tri, hi.astype(BF16), preferred_element_type=F32)
           + jnp.dot(tri, mid.astype(BF16), preferred_element_type=F32)
           + jnp.dot(tri, lo.astype(BF16), preferred_element_type=F32))

    @pl.when(i % tiles_per_seq == 0)
    def _():
        carry_ref[...] = jnp.zeros_like(carry_ref)

    decay = cum + carry_ref[0:1, :]
    carry_ref[0:1, :] = decay[tm - 1:tm, :]
    d_hi, d_mid, d_lo = _split3(decay * LOG2E)
    lane = lax.broadcasted_iota(jnp.int32, (1, LANES), 1)
    l8 = lane % AUG_SLOT
    valid = lane < FOX_HEADS * AUG_SLOT
    one = jnp.ones_like(d_hi)
    zero = jnp.zeros_like(d_hi)
    qaug = jnp.where(l8 == 0, d_hi, jnp.where(l8 == 1, d_mid, jnp.where(l8 == 2, d_lo,
                     jnp.where(l8 < 6, one, zero))))
    kaug = jnp.where(l8 < 3, one, jnp.where(l8 == 3, -d_hi, jnp.where(l8 == 4, -d_mid,
                     jnp.where(l8 == 5, -d_lo, zero))))
    qaug_ref[...] = jnp.where(valid, qaug, zero).astype(BF16)
    kaug_ref[...] = jnp.where(valid, kaug, zero).astype(BF16)

    gates_ref[:, 0:D_MODEL] = jax.nn.sigmoid(seg("ga")).astype(BF16)
    gates_ref[:, D_MODEL:2 * D_MODEL] = jax.nn.sigmoid(seg("gb")).astype(BF16)


def _attn_kernel(*refs, fox, heads, tiles, seq):
    if fox:
        q_ref, k_ref, qaug_ref, kaug_ref, vt_ref, o_ref, m_scr, acc_scr, s_scr = refs
    else:
        q_ref, k_ref, vt_ref, o_ref, m_scr, acc_scr, s_scr = refs
    assert tiles % 2 == 0
    grp = pl.program_id(1)
    tq = tiles * ATT_TILE
    nq = seq // tq
    lane = lax.broadcasted_iota(jnp.int32, (1, LANES), 1)
    nt = (((1,), (1,)), ((), ()))
    chains = [(h, t) for h in range(heads) for t in range(tiles)]

    def q_block(i, _):
        q0 = pl.multiple_of(i * tq, tq)
        qops = {}
        for h in range(heads):
            for t in range(tiles):
                rows = pl.ds(pl.multiple_of(q0 + t * ATT_TILE, ATT_TILE), ATT_TILE)
                if fox:
                    blk = slice((h // 2) * LANES, (h // 2 + 1) * LANES)
                    qp = q_ref[rows, blk]
                    qa = qaug_ref[rows, :]
                    zq = jnp.zeros_like(qp)
                    qh = jnp.where(lane // FOX_HEAD_DIM == h % 2, qp, zq)
                    qah = jnp.where(lane // AUG_SLOT == heads * grp + h, qa, zq)
                    qops[(h, t)] = jnp.concatenate([qh, qah], axis=1)
                else:
                    qops[(h, t)] = q_ref[rows, h * HEAD_SLOT:(h + 1) * HEAD_SLOT]
        for c in range(len(chains)):
            m_scr[c] = jnp.full((8, ATT_TILE), NEG_INF, F32)
            acc_scr[c] = jnp.zeros((ACC_ROWS, ATT_TILE), F32)
        ones_rows = jnp.ones((ACC_ROWS - 64, ATT_TILE), BF16)

        def logits(j, slot, min_tile):
            krows = pl.ds(pl.multiple_of(j * ATT_TILE, ATT_TILE), ATT_TILE)
            kops = {}
            for h in range(heads):
                if fox:
                    if h % 2 == 0:
                        blk = slice((h // 2) * LANES, (h // 2 + 1) * LANES)
                        kops[h] = jnp.concatenate([k_ref[krows, blk], kaug_ref[krows, :]], axis=1)
                    else:
                        kops[h] = kops[h - 1]
                else:
                    kops[h] = k_ref[krows, h * HEAD_SLOT:(h + 1) * HEAD_SLOT]
            for c, ht in enumerate(chains):
                if ht[1] >= min_tile:
                    s_scr[slot, c] = lax.dot_general(kops[ht[0]], qops[ht], nt,
                                                     preferred_element_type=F32)

        def consume(j, slot, diag):
            active = [(c, ht) for c, ht in enumerate(chains) if diag is None or ht[1] >= diag]
            if diag is not None:
                kpos = lax.broadcasted_iota(jnp.int32, (ATT_TILE, 1), 0)
                qpos = lax.broadcasted_iota(jnp.int32, (1, ATT_TILE), 1)
                allowed = (kpos <= qpos) if fox else ((kpos // CHUNK) <= (qpos // CHUNK))
            p_all, alpha_all = {}, {}
            for c, ht in active:
                s = s_scr[slot, c]
                if diag is not None and ht[1] == diag:
                    s = jnp.where(allowed, s, NEG_INF)
                m = m_scr[c][0:1, :]
                m_new = jnp.maximum(m, jnp.max(s, axis=0, keepdims=True))
                alpha_all[c] = jnp.exp2(m - m_new)
                p_all[c] = jnp.exp2(s - m_new).astype(BF16)
                m_scr[c] = jnp.broadcast_to(m_new, (8, ATT_TILE))
            for c, ht in active:
                vt = jnp.concatenate([vt_ref[j, ht[0] * 64:(ht[0] + 1) * 64, :], ones_rows], axis=0)
                pv = jnp.dot(vt, p_all[c], preferred_element_type=F32)
                acc_scr[c] = alpha_all[c] * acc_scr[c] + pv

        n_full = i * tiles
        logits(0, 0, 0)

        def two_steps(u, _):
            j = 2 * u
            logits(j + 1, 1, 0)
            consume(j, 0, None)
            logits(j + 2, 0, 0)
            consume(j + 1, 1, None)
            return 0

        lax.fori_loop(0, n_full // 2, two_steps, 0)
        for d in range(tiles):
            if d + 1 < tiles:
                logits(n_full + d + 1, (d + 1) % 2, d + 1)
            consume(n_full + d, d % 2, d)
        for t in range(tiles):
            rows = pl.ds(pl.multiple_of(q0 + t * ATT_TILE, ATT_TILE), ATT_TILE)
            for hp in range(heads // 2):
                parts = []
                for h in (2 * hp, 2 * hp + 1):
                    acc = acc_scr[chains.index((h, t))]
                    parts.append(acc[0:64, :] / acc[64:65, :])
                o_t = jnp.concatenate(parts, axis=0)
                o_ref[rows, hp * LANES:(hp + 1) * LANES] = o_t.T.astype(BF16)
        return 0

    lax.fori_loop(0, nq, q_block, 0)


def _post_kernel(x_ref, oa_ref, ob_ref, gates_ref, woa_ref, wob_ref, wout_ref, gffn_ref,
                 wrh_ref, wrl_ref, br_ref, upper_ref, h_ref, hn_ref, meta_ref, wts_ref, cnt_ref, cnt_scr):
    tm = x_ref.shape[0]
    ya = jnp.dot(oa_ref[...], woa_ref[...], preferred_element_type=F32)
    yb = jnp.dot(ob_ref[...], wob_ref[...], preferred_element_type=F32)
    sa = gates_ref[:, 0:D_MODEL].astype(F32)
    sb = gates_ref[:, D_MODEL:2 * D_MODEL].astype(F32)
    mixed = (sa * ya + sb * yb).astype(BF16)
    h = x_ref[...] + jnp.dot(mixed, wout_ref[...], preferred_element_type=F32)
    h_ref[...] = h
    hn = (h * lax.rsqrt(jnp.mean(h * h, axis=-1, keepdims=True) + EPS)) * gffn_ref[...]
    hn_hi = hn.astype(BF16)
    hn_lo = (hn - hn_hi.astype(F32)).astype(BF16)
    for s in range(ROW_CHUNKS):
        hn_ref[pl.ds(s, tm, stride=ROW_CHUNKS), :] = hn[:, s * LANES:(s + 1) * LANES]

    nt = (((1,), (1,)), ((), ()))
    wrh = wrh_ref[...]
    lt = (lax.dot_general(wrh, hn_hi, nt, preferred_element_type=F32)
          + lax.dot_general(wrh, hn_lo, nt, preferred_element_type=F32)
          + lax.dot_general(wrl_ref[...], hn_hi, nt, preferred_element_type=F32)) + br_ref[...]

    row8 = lax.broadcasted_iota(jnp.int32, (EXPERTS_PER_GROUP, tm), 0).astype(F32)
    big = jnp.float32(EXPERTS_PER_GROUP)
    lg = lt[N_EXPERTS:N_EXPERTS + 8, :]
    gmax = jnp.max(lg, axis=0, keepdims=True)
    g_p = 1.0 / jnp.sum(jnp.exp(lg - gmax), axis=0, keepdims=True)
    g_idx = jnp.min(jnp.where(lg == gmax, row8, big), axis=0, keepdims=True)
    sel = jnp.zeros((EXPERTS_PER_GROUP, tm), F32)
    for g in range(N_GROUPS):
        sel = jnp.where(g_idx == g, lt[g * EXPERTS_PER_GROUP:(g + 1) * EXPERTS_PER_GROUP, :], sel)
    m1 = jnp.max(sel, axis=0, keepdims=True)
    i1 = jnp.min(jnp.where(sel == m1, row8, big), axis=0, keepdims=True)
    sel2 = jnp.where(row8 == i1, -jnp.inf, sel)
    m2 = jnp.max(sel2, axis=0, keepdims=True)
    i2 = jnp.min(jnp.where(sel2 == m2, row8, big), axis=0, keepdims=True)
    r = jnp.exp(m2 - m1)
    w1 = g_p / (1.0 + r)
    w2 = g_p * r / (1.0 + r)
    e1 = g_idx * EXPERTS_PER_GROUP + i1
    e2 = g_idx * EXPERTS_PER_GROUP + i2
    @pl.when(pl.program_id(0) == 0)
    def _():
        cnt_scr[...] = jnp.zeros_like(cnt_scr)

    rowe = lax.broadcasted_iota(jnp.int32, (N_EXPERTS, tm), 0).astype(F32)
    is1 = rowe == e1
    is2 = rowe == e2
    onehot = jnp.where(is1 | is2, 1.0, 0.0)
    before = jnp.dot(onehot.astype(BF16), upper_ref[...], preferred_element_type=F32) + cnt_scr[:, 0:1]
    r1 = jnp.sum(jnp.where(is1, before, 0.0), axis=0, keepdims=True)
    r2 = jnp.sum(jnp.where(is2, before, 0.0), axis=0, keepdims=True)
    cnt_scr[...] = cnt_scr[...] + jnp.sum(onehot, axis=1, keepdims=True)
    cnt_ref[...] = cnt_scr[...]
    zrow = jnp.zeros_like(e1)
    meta_ref[0] = jnp.concatenate([e1, e2, r1, r2, zrow, zrow, zrow, zrow], axis=0).astype(jnp.int32)
    row = lax.broadcasted_iota(jnp.int32, (LANES, tm), 0)
    wts_t = jnp.where(row == 0, w1, jnp.where(row == 1, w2, 0.0))
    wts_ref[...] = wts_t.T


def _dispatch_kernel(off_ref, cnt_ref, nused_ref, pos_hbm, hn_ref, xs_hbm, pos_smem, zbuf, sem_meta,
                     sem_zero, sem_rows, *, tm, tm_e, n_tiles_e):
    i = pl.program_id(0)
    meta_cp = pltpu.make_async_copy(pos_hbm.at[pl.ds(pl.multiple_of(i * 2 * tm, 2 * tm), 2 * tm)],
                                    pos_smem, sem_meta)
    meta_cp.start()

    @pl.when(i == 0)
    def _():
        zbuf[...] = jnp.zeros_like(zbuf)
        shift = tm_e.bit_length() - 1

        def unused_tile(t, _):
            cp = pltpu.make_async_copy(zbuf, xs_hbm.at[pl.ds(pl.multiple_of(t * tm_e, tm_e), tm_e)], sem_zero)
            cp.start()
            cp.wait()
            return 0

        lax.fori_loop(nused_ref[0], n_tiles_e, unused_tile, 0)

        def tail_copy(e):
            n = cnt_ref[e]
            start = off_ref[e] + lax.shift_left(lax.shift_right_logical(jnp.maximum(n, 1) - 1, shift), shift)
            return n, pltpu.make_async_copy(zbuf, xs_hbm.at[pl.ds(pl.multiple_of(start, tm_e), tm_e)], sem_zero)

        for e in range(N_EXPERTS):
            n, cp = tail_copy(e)

            @pl.when(n > 0)
            def _():
                cp.start()
        for e in range(N_EXPERTS):
            n, cp = tail_copy(e)

            @pl.when(n > 0)
            def _():
                cp.wait()

    meta_cp.wait()

    def row_copies(t):
        src = hn_ref.at[t]
        return (pltpu.make_async_copy(src, xs_hbm.at[pos_smem[t]], sem_rows),
                pltpu.make_async_copy(src, xs_hbm.at[pos_smem[tm + t]], sem_rows))

    def issue(t, _):
        for prio, cp in enumerate(row_copies(t)):
            cp.start(priority=prio)
        return 0

    def drain(t, _):
        for cp in row_copies(t):
            cp.wait()
        return 0

    lax.fori_loop(0, tm, issue, 0, unroll=ROW_DMA_UNROLL)
    lax.fori_loop(0, tm, drain, 0, unroll=ROW_DMA_UNROLL)


def _moe_kernel(texp_ref, nused_ref, xs_ref, wg_ref, wu_ref, wd_ref, ys_ref):
    del texp_ref

    @pl.when(pl.program_id(0) >= nused_ref[0])
    def _():
        ys_ref[...] = jnp.zeros_like(ys_ref)

    @pl.when(pl.program_id(0) < nused_ref[0])
    def _():
        tm_e = xs_ref.shape[0] // ROW_CHUNKS
        x = jnp.concatenate([xs_ref[pl.ds(s, tm_e, stride=ROW_CHUNKS), :] for s in range(ROW_CHUNKS)],
                            axis=1).astype(BF16)
        g = jnp.dot(x, wg_ref[0].astype(BF16), preferred_element_type=F32)
        u = jnp.dot(x, wu_ref[0].astype(BF16), preferred_element_type=F32)
        a = (jax.nn.silu(g) * u).astype(BF16)
        y = jnp.dot(a, wd_ref[0].astype(BF16), preferred_element_type=F32)
        for s in range(ROW_CHUNKS):
            ys_ref[pl.ds(s, tm_e, stride=ROW_CHUNKS), :] = y[:, s * LANES:(s + 1) * LANES]


def _combine_kernel(pos_hbm, ys_hbm, h_ref, wts_ref, gfin_ref, out_ref, pos_smem, y1_buf, y2_buf,
                    sem_meta, sem_rows, *, tm):
    i = pl.program_id(0)
    meta_cp = pltpu.make_async_copy(pos_hbm.at[pl.ds(pl.multiple_of(i * 2 * tm, 2 * tm), 2 * tm)],
                                    pos_smem, sem_meta)
    meta_cp.start()
    meta_cp.wait()

    def row_copies(t):
        dst = pl.ds(pl.multiple_of(t * ROW_CHUNKS, ROW_CHUNKS), ROW_CHUNKS)
        return (pltpu.make_async_copy(ys_hbm.at[pos_smem[t]], y1_buf.at[dst], sem_rows),
                pltpu.make_async_copy(ys_hbm.at[pos_smem[tm + t]], y2_buf.at[dst], sem_rows))

    def issue(t, _):
        for prio, cp in enumerate(row_copies(t)):
            cp.start(priority=prio)
        return 0

    def drain(t, _):
        for cp in row_copies(t):
            cp.wait()
        return 0

    lax.fori_loop(0, tm, issue, 0, unroll=ROW_DMA_UNROLL)
    lax.fori_loop(0, tm, drain, 0, unroll=ROW_DMA_UNROLL)
    w = wts_ref[...]
    w1 = w[:, 0:1]
    w2 = w[:, 1:2]
    y = jnp.concatenate([w1 * y1_buf[pl.ds(s, tm, stride=ROW_CHUNKS), :]
                         + w2 * y2_buf[pl.ds(s, tm, stride=ROW_CHUNKS), :] for s in range(ROW_CHUNKS)], axis=1)
    hh = h_ref[...] + y
    out_ref[...] = (hh * lax.rsqrt(jnp.mean(hh * hh, axis=-1, keepdims=True) + EPS)) * gfin_ref[...]


def _pad_heads(w, heads, dim):
    k = w.shape[0]
    w = w.reshape(k, heads, dim)
    return jnp.pad(w, ((0, 0), (0, 0), (0, HEAD_SLOT - dim))).reshape(k, heads * HEAD_SLOT)


def _swap_rope(w_rope):
    half = MLA_ROPE_DIM // 2
    return jnp.concatenate([-w_rope[..., half:], w_rope[..., :half]], axis=-1)


def _params(sem):
    return pltpu.CompilerParams(dimension_semantics=sem, vmem_limit_bytes=VMEM_LIMIT_BYTES)


def kernel(x, positions, g_mix, w_in, b_forget, g_cq, g_ckv, w_uq, w_uk, w_uv, w_o_mla, w_o_fox,
           w_out, g_ffn, w_group, b_group, w_router, b_router, w_e_gate, w_e_up, w_e_down, g_final):
    B, S, D = x.shape
    T = B * S
    assert D == D_MODEL
    tm = min(512, S)
    tk = ATT_TILE
    att_heads = 4
    att_tiles = 4
    tm_e = 512
    tm_d = min(1024, T)
    assert T % tm_d == 0
    assert S % tm == 0 and S % (att_tiles * ATT_TILE) == 0 and tm % tk == 0 and tk % CHUNK == 0

    cq_w, ckv_w, kr_w, fq_w, fk_w, fv_w, fl_w, ga_w, gb_w = jnp.split(
        w_in, [256, 384, 416, 928, 1440, 1952, 1960, 2984], axis=1)
    z64 = jnp.zeros((D, MLA_NOPE_DIM), F32)
    z32 = jnp.zeros((D, HEAD_SLOT - MLA_NOPE_DIM - MLA_ROPE_DIM), F32)
    fl_rep = jnp.pad(jnp.repeat(fl_w, AUG_SLOT, axis=1), ((0, 0), (0, LANES - FOX_HEADS * AUG_SLOT)))
    w1 = jnp.concatenate([cq_w, ckv_w, fl_rep,
                          jnp.concatenate([z64, kr_w, z32], axis=1),
                          jnp.concatenate([z64, _swap_rope(kr_w), z32], axis=1),
                          fq_w, fk_w, fv_w, ga_w, gb_w], axis=1).astype(BF16)
    assert w1.shape[1] == W1_COLS
    bf128 = jnp.pad(jnp.repeat(b_forget.astype(F32), AUG_SLOT), (0, LANES - FOX_HEADS * AUG_SLOT))[None, :]
    dq = MLA_NOPE_DIM + MLA_ROPE_DIM
    wq3 = w_uq.reshape(MLA_Q_RANK, MLA_HEADS, dq)
    wq_main = _pad_heads(w_uq, MLA_HEADS, dq).astype(BF16)
    wq_swap3 = jnp.concatenate([jnp.zeros((MLA_Q_RANK, MLA_HEADS, MLA_NOPE_DIM), F32),
                                _swap_rope(wq3[..., MLA_NOPE_DIM:])], axis=-1)
    wq_swap = _pad_heads(wq_swap3.reshape(MLA_Q_RANK, MLA_HEADS * dq), MLA_HEADS, dq).astype(BF16)
    wuk_pad = _pad_heads(w_uk, MLA_HEADS, MLA_NOPE_DIM).astype(BF16)
    half = MLA_ROPE_DIM // 2
    inv_freq = ROPE_THETA ** (-jnp.arange(half, dtype=F32) / half)
    freq128 = jnp.concatenate([jnp.zeros((MLA_NOPE_DIM,), F32), inv_freq, inv_freq,
                               jnp.zeros((HEAD_SLOT - dq,), F32)])[None, :]
    tri = (jnp.arange(tm)[:, None] >= jnp.arange(tm)[None, :]).astype(BF16)

    x2 = x.reshape(T, D)
    pos2 = positions.reshape(T, 1).astype(jnp.int32)
    row = lambda v: v.astype(F32)[None, :]

    full = lambda shape: pl.BlockSpec(shape, lambda i: (0,) * len(shape))
    tok = lambda w: pl.BlockSpec((tm, w), lambda i: (i, 0))
    n_tiles = T // tm
    nkv = S // tk

    q, k, vt, fq, fk, qaug, kaug, gates = pl.pallas_call(
        functools.partial(_proj_kernel, tiles_per_seq=S // tm, tk=tk,
                          scale_a=float(dq) ** -0.5, scale_b=float(FOX_HEAD_DIM) ** -0.5),
        grid=(n_tiles,),
        in_specs=[tok(D), tok(1), full((1, D)), full((D, W1_COLS)), full((1, MLA_Q_RANK)),
                  full((1, MLA_KV_RANK)), full((MLA_Q_RANK, MLA_HEADS * HEAD_SLOT)),
                  full((MLA_Q_RANK, MLA_HEADS * HEAD_SLOT)), full((MLA_KV_RANK, MLA_HEADS * HEAD_SLOT)),
                  full((MLA_KV_RANK, 512)), full((1, LANES)), full((1, LANES)), full((tm, tm))],
        out_specs=[tok(1024), tok(1024),
                   pl.BlockSpec((tm // tk, 1024, tk), lambda i: (i, 0, 0)),
                   tok(512), tok(512), tok(LANES), tok(LANES), tok(2 * D)],
        out_shape=[jax.ShapeDtypeStruct((T, 1024), BF16), jax.ShapeDtypeStruct((T, 1024), BF16),
                   jax.ShapeDtypeStruct((T // tk, 1024, tk), BF16),
                   jax.ShapeDtypeStruct((T, 512), BF16), jax.ShapeDtypeStruct((T, 512), BF16),
                   jax.ShapeDtypeStruct((T, LANES), BF16), jax.ShapeDtypeStruct((T, LANES), BF16),
                   jax.ShapeDtypeStruct((T, 2 * D), BF16)],
        scratch_shapes=[pltpu.VMEM((8, LANES), F32)],
        compiler_params=_params(("arbitrary",)),
        name="proj",
    )(x2, pos2, row(g_mix), w1, row(g_cq), row(g_ckv), wq_main, wq_swap, wuk_pad,
      w_uv.astype(BF16), bf128, freq128, tri)

    n_grp = MLA_HEADS // att_heads
    n_chains = att_heads * att_tiles
    seq_blk = lambda w: pl.BlockSpec((S, w), lambda b, g: (b, g))
    seq_all = lambda w: pl.BlockSpec((S, w), lambda b, g: (b, 0))
    att_scratch = [pltpu.VMEM((n_chains, 8, ATT_TILE), F32), pltpu.VMEM((n_chains, ACC_ROWS, ATT_TILE), F32),
                   pltpu.VMEM((2, n_chains, ATT_TILE, ATT_TILE), F32)]
    o_a = pl.pallas_call(
        functools.partial(_attn_kernel, fox=False, heads=att_heads, tiles=att_tiles, seq=S),
        grid=(B, n_grp),
        in_specs=[seq_blk(att_heads * HEAD_SLOT), seq_blk(att_heads * HEAD_SLOT),
                  pl.BlockSpec((nkv, att_heads * 64, tk), lambda b, g: (b, g, 0))],
        out_specs=seq_blk(att_heads * 64),
        out_shape=jax.ShapeDtypeStruct((T, 512), BF16),
        scratch_shapes=att_scratch,
        compiler_params=_params(("arbitrary", "arbitrary")),
        name="attn_mla",
    )(q, k, vt)
    o_b = pl.pallas_call(
        functools.partial(_attn_kernel, fox=True, heads=att_heads, tiles=att_tiles, seq=S),
        grid=(B, n_grp),
        in_specs=[seq_blk(att_heads * 64), seq_blk(att_heads * 64), seq_all(LANES), seq_all(LANES),
                  pl.BlockSpec((nkv, att_heads * 64, tk), lambda b, g: (b, n_grp + g, 0))],
        out_specs=seq_blk(att_heads * 64),
        out_shape=jax.ShapeDtypeStruct((T, 512), BF16),
        scratch_shapes=att_scratch,
        compiler_params=_params(("arbitrary", "arbitrary")),
        name="attn_fox",
    )(fq, fk, qaug, kaug, vt)

    n_rt = 64
    wr = jnp.concatenate([w_router.T, w_group.T,
                          jnp.zeros((n_rt - N_EXPERTS - N_GROUPS, D), F32)], axis=0).astype(F32)
    wr_hi = wr.astype(BF16)
    wr_lo = (wr - wr_hi.astype(F32)).astype(BF16)
    br = jnp.concatenate([b_router.astype(F32), b_group.astype(F32),
                          jnp.full((n_rt - N_EXPERTS - N_GROUPS,), NEG_INF, F32)])[:, None]
    upper = (jnp.arange(tm)[:, None] < jnp.arange(tm)[None, :]).astype(BF16)
    h, hn, meta, wts, counts = pl.pallas_call(
        _post_kernel,
        grid=(n_tiles,),
        in_specs=[tok(D), tok(512), tok(512), tok(2 * D), full((512, D)), full((512, D)),
                  full((D, D)), full((1, D)), full((n_rt, D)), full((n_rt, D)), full((n_rt, 1)),
                  full((tm, tm))],
        out_specs=[tok(D), pl.BlockSpec((tm * ROW_CHUNKS, LANES), lambda i: (i, 0)),
                   pl.BlockSpec((1, 8, tm), lambda i: (i, 0, 0)), tok(LANES), full((N_EXPERTS, LANES))],
        out_shape=[jax.ShapeDtypeStruct((T, D), F32), jax.ShapeDtypeStruct((T * ROW_CHUNKS, LANES), F32),
                   jax.ShapeDtypeStruct((n_tiles, 8, tm), jnp.int32),
                   jax.ShapeDtypeStruct((T, LANES), F32),
                   jax.ShapeDtypeStruct((N_EXPERTS, LANES), F32)],
        scratch_shapes=[pltpu.VMEM((N_EXPERTS, LANES), F32)],
        compiler_params=_params(("arbitrary",)),
        name="post",
    )(x2, o_a, o_b, gates, w_o_mla.astype(BF16), w_o_fox.astype(BF16), w_out.astype(BF16),
      row(g_ffn), wr_hi, wr_lo, br, upper)

    n_tiles_e = (2 * T) // tm_e + N_EXPERTS
    cnt = counts[:, 0].astype(jnp.int32)
    tiles_per_e = (cnt + (tm_e - 1)) // tm_e
    cum_tiles = jnp.cumsum(tiles_per_e)
    off = ((cum_tiles - tiles_per_e) * tm_e).astype(jnp.int32)
    n_used = cum_tiles[-1:].astype(jnp.int32)
    tile_ids = jnp.minimum(jnp.arange(n_tiles_e, dtype=jnp.int32), n_used[0] - 1)
    tile_expert = jnp.sum(tile_ids[:, None] >= cum_tiles[None, :], axis=1).astype(jnp.int32)

    experts = jnp.arange(N_EXPERTS, dtype=jnp.int32)
    run_start = lambda e: jnp.sum(jnp.where(e[..., None] == experts, off, 0), axis=-1)
    pos = jnp.stack([(run_start(meta[:, 0, :]) + meta[:, 2, :]).reshape(T // tm_d, tm_d),
                     (run_start(meta[:, 1, :]) + meta[:, 3, :]).reshape(T // tm_d, tm_d)],
                    axis=1).reshape(-1)

    any_spec = pl.BlockSpec(memory_space=pl.ANY)
    dma_sem = pltpu.SemaphoreType.DMA(())
    row_tile = (ROW_CHUNKS, LANES)
    n_rows_e = n_tiles_e * tm_e
    xs = pl.pallas_call(
        functools.partial(_dispatch_kernel, tm=tm_d, tm_e=tm_e, n_tiles_e=n_tiles_e),
        grid_spec=pltpu.PrefetchScalarGridSpec(
            num_scalar_prefetch=3, grid=(T // tm_d,),
            in_specs=[any_spec, pl.BlockSpec((tm_d,) + row_tile, lambda i, *_: (i, 0, 0))],
            out_specs=any_spec,
            scratch_shapes=[pltpu.SMEM((2 * tm_d,), jnp.int32), pltpu.VMEM((tm_e,) + row_tile, F32),
                            dma_sem, dma_sem, dma_sem]),
        out_shape=jax.ShapeDtypeStruct((n_rows_e,) + row_tile, F32),
        compiler_params=_params(("arbitrary",)),
        name="dispatch",
    )(off, cnt, n_used, pos, hn.reshape((T,) + row_tile))

    used = lambda i, texp, nused: (jnp.minimum(i, nused[0] - 1), 0)
    ys = pl.pallas_call(
        _moe_kernel,
        grid_spec=pltpu.PrefetchScalarGridSpec(
            num_scalar_prefetch=2, grid=(n_tiles_e,),
            in_specs=[pl.BlockSpec((tm_e * ROW_CHUNKS, LANES), used),
                      pl.BlockSpec((1, D, EXPERT_FF), lambda i, texp, nused: (texp[i], 0, 0)),
                      pl.BlockSpec((1, D, EXPERT_FF), lambda i, texp, nused: (texp[i], 0, 0)),
                      pl.BlockSpec((1, EXPERT_FF, D), lambda i, texp, nused: (texp[i], 0, 0))],
            out_specs=pl.BlockSpec((tm_e * ROW_CHUNKS, LANES), lambda i, texp, nused: (i, 0))),
        out_shape=jax.ShapeDtypeStruct((n_rows_e * ROW_CHUNKS, LANES), F32),
        compiler_params=_params(("arbitrary",)),
        name="moe",
    )(tile_expert, n_used, xs.reshape(n_rows_e * ROW_CHUNKS, LANES), w_e_gate, w_e_up, w_e_down)

    out = pl.pallas_call(
        functools.partial(_combine_kernel, tm=tm_d),
        grid=(T // tm_d,),
        in_specs=[any_spec, any_spec, pl.BlockSpec((tm_d, D), lambda i: (i, 0)),
                  pl.BlockSpec((tm_d, LANES), lambda i: (i, 0)), full((1, D))],
        out_specs=pl.BlockSpec((tm_d, D), lambda i: (i, 0)),
        scratch_shapes=[pltpu.SMEM((2 * tm_d,), jnp.int32), pltpu.VMEM((tm_d * ROW_CHUNKS, LANES), F32),
                        pltpu.VMEM((tm_d * ROW_CHUNKS, LANES), F32), dma_sem, dma_sem],
        out_shape=jax.ShapeDtypeStruct((T, D), F32),
        compiler_params=_params(("arbitrary",)),
        name="combine",
    )(pos, ys.reshape((n_rows_e,) + row_tile), h, wts, row(g_final))
    return out.reshape(B, S, D)
```
